```python
import math, functools
import jax, jax.numpy as jnp
from jax import lax
import numpy as np

D_MODEL = 2048
BATCH = 4
SEQ = 2048
DEPTH = 1
DEC_BATCH = 128
DEC_SEQ = 8
PAST_LEN = 16384
PAGE_SIZE = 128

V_HEAD = 128
MLA_HEADS = D_MODEL // 2 // V_HEAD
MLA_WIDTH = MLA_HEADS * V_HEAD
QK_NOPE = 128
QK_ROPE = 64
Q_LORA = 512
KV_LORA = 512
ROPE_BASE = 10000.0
ATTN_SCALE = (QK_NOPE + QK_ROPE) ** -0.5
Q_BLOCK = 128
HG_DK = 128
HG_DV = 128
HG_HEADS = (D_MODEL - MLA_WIDTH) // HG_DV
HG_KW = HG_HEADS * HG_DK
HG_VW = HG_HEADS * HG_DV
HG_CHUNK = 64
MIX_WIDTH = MLA_WIDTH + HG_VW
IN_WIDTHS = (Q_LORA, KV_LORA, QK_ROPE, HG_KW, HG_KW, HG_VW, HG_VW)
IN_WIDTH = sum(IN_WIDTHS)
D_FF = -(-8 * D_MODEL // (3 * 256)) * 256
PLE_DIM = 256
EPS = 1e-6

kernel_name = "hybrid_mla_hgrn2_decode_step"


def rms_norm(x, g):
    x32 = x.astype(jnp.float32)
    y = x32 * lax.rsqrt(jnp.mean(x32 * x32, axis=-1, keepdims=True) + EPS)
    return (y * g.astype(jnp.float32)).astype(x.dtype)


def split_in(z):
    outs, start = [], 0
    for w in IN_WIDTHS:
        outs.append(z[..., start:start + w])
        start += w
    return outs


def rope_angles(pos):
    inv = ROPE_BASE ** (-jnp.arange(0, QK_ROPE, 2, dtype=jnp.float32) / QK_ROPE)
    ang = pos.astype(jnp.float32)[:, None] * inv[None, :]
    return jnp.cos(ang), jnp.sin(ang)


def apply_rope(x, cos, sin):
    shape = (1, cos.shape[0]) + (1,) * (x.ndim - 3) + (cos.shape[1],)
    c, s = cos.reshape(shape), sin.reshape(shape)
    x32 = x.astype(jnp.float32)
    half = QK_ROPE // 2
    x1, x2 = x32[..., :half], x32[..., half:]
    return jnp.concatenate([x1 * c - x2 * s, x2 * c + x1 * s], axis=-1).astype(x.dtype)


def mla_project(cq, ckv_raw, kpe_raw, pos, g_q, w_uq, g_kv, w_uk):
    cq = rms_norm(cq, g_q)
    q = jnp.einsum('btr,rhd->bthd', cq, w_uq)
    q_nope, q_pe = q[..., :QK_NOPE], q[..., QK_NOPE:]
    cos, sin = rope_angles(pos)
    q_pe = apply_rope(q_pe, cos, sin)
    k_pe = apply_rope(kpe_raw, cos, sin)
    c_kv = rms_norm(ckv_raw, g_kv)
    q_abs = jnp.einsum('bthn,chn->bthc', q_nope, w_uk)
    return q_abs, q_pe, c_kv, k_pe


def latent_scores(q_abs, q_pe, c_kv, k_pe):
    s = jnp.einsum('...qhc,...kc->...hqk', q_abs, c_kv, preferred_element_type=jnp.float32)
    s = s + jnp.einsum('...qhr,...kr->...hqk', q_pe, k_pe, preferred_element_type=jnp.float32)
    return s * ATTN_SCALE


def mla_prompt(q_abs, q_pe, c_kv, k_pe):
    B, T = q_abs.shape[:2]
    qb = math.gcd(T, Q_BLOCK)
    kpos = jnp.arange(T)

    def block(i):
        qa = lax.dynamic_slice_in_dim(q_abs, i * qb, qb, axis=1)
        qp = lax.dynamic_slice_in_dim(q_pe, i * qb, qb, axis=1)
        s = latent_scores(qa, qp, c_kv, k_pe)
        qpos = i * qb + jnp.arange(qb)
        s = jnp.where(kpos[None, :] <= qpos[:, None], s, -jnp.inf)
        p = jax.nn.softmax(s, axis=-1).astype(c_kv.dtype)
        return jnp.einsum('bhqk,bkc->bqhc', p, c_kv)

    o = lax.map(block, jnp.arange(T // qb))
    return jnp.moveaxis(o, 0, 1).reshape(B, T, MLA_HEADS, KV_LORA)


def mla_sample(q_abs, q_pe, c_kv, k_pe, cache_lat, cache_rope, page_table, layer):
    T = q_abs.shape[1]
    causal = jnp.arange(T)[None, :] <= jnp.arange(T)[:, None]

    def one(args):
        qa, qp, cn, kn, pages = args
        past_c = cache_lat[layer, pages].reshape(-1, KV_LORA)
        past_r = cache_rope[layer, pages].reshape(-1, QK_ROPE)
        s_past = latent_scores(qa, qp, past_c, past_r)
        s_new = jnp.where(causal, latent_scores(qa, qp, cn, kn), -jnp.inf)
        p = jax.nn.softmax(jnp.concatenate([s_past, s_new], axis=-1), axis=-1).astype(cn.dtype)
        L = past_c.shape[0]
        return (jnp.einsum('hqk,kc->qhc', p[..., :L], past_c)
                + jnp.einsum('hqk,kc->qhc', p[..., L:], cn))

    return lax.map(one, (q_abs, q_pe, c_kv, k_pe, page_table))


def hgrn2_gates(f_raw, lb):
    z = f_raw.astype(jnp.float32)
    log_f = jnp.log(lb + (1.0 - lb) * jax.nn.sigmoid(z))
    k = (1.0 - lb) * jax.nn.sigmoid(-z)
    return log_f, k


def hgrn2_scan(q, k, v, log_f, s0):
    B, T = q.shape[:2]
    c = math.gcd(T, HG_CHUNK)
    n = T // c
    out_dtype = v.dtype

    def to_chunks(a):
        a = a.astype(jnp.float32)
        return jnp.moveaxis(a.reshape((B, n, c) + a.shape[2:]), 1, 0)

    tri = jnp.arange(c)[:, None] >= jnp.arange(c)[None, :]

    def step(S, inp):
        qc, kc, vc, gc = inp
        b = jnp.cumsum(gc, axis=1)
        diff = b[:, :, None] - b[:, None, :]
        decay = jnp.exp(jnp.where(tri[None, :, :, None, None], diff, -jnp.inf))
        a = jnp.einsum('bthd,bshd,btshd->bhts', qc, kc, decay)
        o = (jnp.einsum('bhts,bshv->bthv', a, vc)
             + jnp.einsum('bthd,bhdv->bthv', qc * jnp.exp(b), S))
        b_last = b[:, -1]
        kd = kc * jnp.exp(b_last[:, None] - b)
        S = S * jnp.exp(b_last)[..., None] + jnp.einsum('bshd,bshv->bhdv', kd, vc)
        return S, o

    S, o = lax.scan(step, s0.astype(jnp.float32),
                    (to_chunks(q), to_chunks(k), to_chunks(v), to_chunks(log_f)))
    o = jnp.moveaxis(o, 0, 1).reshape(B, T, HG_HEADS, HG_DV).astype(out_dtype)
    return o, S


def layer_forward(x, p, pos, s0, lb, attend, g_mix, w_in, g_q, w_uq, g_kv, w_uk, w_uv,
                  g_mla_out, g_hg_out, w_o, g_ffn, w_ffn_gate, w_ffn_up, w_ffn_down,
                  g_ple, w_ple_gate, w_ple_proj):
    B, T, _ = x.shape
    u = rms_norm(x, g_mix)
    cq, ckv_raw, kpe_raw, hq, hf, hi, hg = split_in(u @ w_in)
    q_abs, q_pe, c_kv, k_pe = mla_project(cq, ckv_raw, kpe_raw, pos, g_q, w_uq, g_kv, w_uk)
    o_lat = attend(q_abs, q_pe, c_kv, k_pe)
    o_mla = jnp.einsum('bthc,chv->bthv', o_lat.astype(x.dtype), w_uv)
    o_mla = rms_norm(o_mla, g_mla_out).reshape(B, T, MLA_WIDTH)
    q = jax.nn.silu(hq).reshape(B, T, HG_HEADS, HG_DK)
    log_f, k = hgrn2_gates(hf, lb)
    o_hg, s_new = hgrn2_scan(q, k.reshape(B, T, HG_HEADS, HG_DK),
                             hi.reshape(B, T, HG_HEADS, HG_DV),
                             log_f.reshape(B, T, HG_HEADS, HG_DK), s0)
    o_hg = rms_norm(o_hg, g_hg_out) * jax.nn.silu(hg.reshape(B, T, HG_HEADS, HG_DV))
    o_hg = o_hg.reshape(B, T, HG_VW)
    h = x + jnp.concatenate([o_mla, o_hg], axis=-1) @ w_o
    hn = rms_norm(h, g_ffn)
    h = h + (jax.nn.silu(hn @ w_ffn_gate) * (hn @ w_ffn_up)) @ w_ffn_down
    h = h + (p @ w_ple_proj) * jax.nn.sigmoid(rms_norm(h, g_ple) @ w_ple_gate)
    return h, c_kv, k_pe, s_new


def setup_inputs(seed: int = 0) -> dict:
    key = jax.random.key(seed)
    ks = iter(jax.random.split(key, 32))
    f32 = jnp.float32

    def nrm(shape, scale):
        return jax.random.normal(next(ks), shape, f32) * scale

    def gain(shape):
        return 1.0 + nrm(shape, 0.02)

    n_pages = PAST_LEN // PAGE_SIZE
    n_used = DEC_BATCH * n_pages
    n_pool = n_used + (n_used + 3) // 4
    page_table = jax.random.permutation(next(ks), n_pool)[:n_used].reshape(DEC_BATCH, n_pages).astype(jnp.int32)
    return {
        "x_prompt": nrm((BATCH, SEQ, D_MODEL), 1.0),
        "x_sample": nrm((DEC_BATCH, DEC_SEQ, D_MODEL), 1.0),
        "cache_kv_latent": nrm((DEPTH, n_pool, PAGE_SIZE, KV_LORA), 1.0),
        "cache_k_rope": nrm((DEPTH, n_pool, PAGE_SIZE, QK_ROPE), 1.0),
        "state_hgrn": nrm((DEPTH, DEC_BATCH, HG_HEADS, HG_DK, HG_DV), 0.5),
        "page_table": page_table,
        "p_prompt": nrm((DEPTH, BATCH, SEQ, PLE_DIM), 1.0),
        "p_sample": nrm((DEPTH, DEC_BATCH, DEC_SEQ, PLE_DIM), 1.0),
        "g_mix": gain((DEPTH, D_MODEL)),
        "w_in": nrm((DEPTH, D_MODEL, IN_WIDTH), D_MODEL ** -0.5),
        "g_q": gain((DEPTH, Q_LORA)),
        "w_uq": nrm((DEPTH, Q_LORA, MLA_HEADS, QK_NOPE + QK_ROPE), Q_LORA ** -0.5),
        "g_kv": gain((DEPTH, KV_LORA)),
        "w_uk": nrm((DEPTH, KV_LORA, MLA_HEADS, QK_NOPE), KV_LORA ** -0.5),
        "w_uv": nrm((DEPTH, KV_LORA, MLA_HEADS, V_HEAD), KV_LORA ** -0.5),
        "g_mla_out": gain((DEPTH, V_HEAD)),
        "hg_lb": nrm((DEPTH + 1, HG_KW), 0.5),
        "g_hg_out": gain((DEPTH, HG_DV)),
        "w_o": nrm((DEPTH, MIX_WIDTH, D_MODEL), MIX_WIDTH ** -0.5),
        "g_ffn": gain((DEPTH, D_MODEL)),
        "w_ffn_gate": nrm((DEPTH, D_MODEL, D_FF), D_MODEL ** -0.5),
        "w_ffn_up": nrm((DEPTH, D_MODEL, D_FF), D_MODEL ** -0.5),
        "w_ffn_down": nrm((DEPTH, D_FF, D_MODEL), D_FF ** -0.5),
        "g_ple": gain((DEPTH, D_MODEL)),
        "w_ple_gate": nrm((DEPTH, D_MODEL, D_MODEL), D_MODEL ** -0.5),
        "w_ple_proj": nrm((DEPTH, PLE_DIM, D_MODEL), PLE_DIM ** -0.5),
        "g_final": gain((D_MODEL,)),
    }


def reference(x_prompt, x_sample, cache_kv_latent, cache_k_rope, state_hgrn, page_table,
              p_prompt, p_sample, g_mix, w_in, g_q, w_uq, g_kv, w_uk, w_uv, g_mla_out,
              hg_lb, g_hg_out, w_o, g_ffn, w_ffn_gate, w_ffn_up, w_ffn_down, g_ple,
              w_ple_gate, w_ple_proj, g_final):
    lb_all = jnp.cumsum(jax.nn.softmax(hg_lb.astype(jnp.float32), axis=0), axis=0)
    past_len = page_table.shape[1] * PAGE_SIZE
    pos_p = jnp.arange(x_prompt.shape[1], dtype=jnp.int32)
    pos_s = past_len + jnp.arange(x_sample.shape[1], dtype=jnp.int32)
    hp, hs = x_prompt, x_sample
    lat_p, rope_p, st_p, lat_s, rope_s, st_s = [], [], [], [], [], []
    for i in range(DEPTH):
        w = (g_mix[i], w_in[i], g_q[i], w_uq[i], g_kv[i], w_uk[i], w_uv[i], g_mla_out[i],
             g_hg_out[i], w_o[i], g_ffn[i], w_ffn_gate[i], w_ffn_up[i], w_ffn_down[i],
             g_ple[i], w_ple_gate[i], w_ple_proj[i])
        s0_p = jnp.zeros((hp.shape[0], HG_HEADS, HG_DK, HG_DV), jnp.float32)
        hp, c, r, s = layer_forward(hp, p_prompt[i], pos_p, s0_p, lb_all[i], mla_prompt, *w)
        lat_p.append(c); rope_p.append(r); st_p.append(s)
        attend_s = functools.partial(mla_sample, cache_lat=cache_kv_latent,
                                     cache_rope=cache_k_rope, page_table=page_table, layer=i)
        hs, c, r, s = layer_forward(hs, p_sample[i], pos_s, state_hgrn[i], lb_all[i], attend_s, *w)
        lat_s.append(c); rope_s.append(r); st_s.append(s)
    y_prompt = rms_norm(hp, g_final)
    y_sample = rms_norm(hs, g_final)
    return (y_prompt, y_sample, jnp.stack(lat_p), jnp.stack(rope_p), jnp.stack(st_p),
            jnp.stack(lat_s), jnp.stack(rope_s), jnp.stack(st_s))
```

```python
import functools
import math

import jax
import jax.numpy as jnp
from jax import lax
from jax.experimental import pallas as pl
from jax.experimental.pallas import tpu as pltpu

F32 = jnp.float32
BF16 = jnp.bfloat16

EPS = 1e-6
ROPE_BASE = 10000.0
PAGE_SIZE = 128
MLA_HEADS = 8
V_HEAD = 128
QK_NOPE = 128
QK_ROPE = 64
Q_LORA = 512
KV_LORA = 512
ATTN_SCALE = (QK_NOPE + QK_ROPE) ** -0.5
HG_HEADS = 8
HG_DK = 128
HG_DV = 128
HG_CHUNK = 64
HG_SUB = 16

V7X_VMEM_BYTES = 64 * 1024 * 1024
LANES = 128

_NT = (((1,), (1,)), ((), ()))
_TN = (((0,), (0,)), ((), ()))


def _vmem_limit(nbytes):
    return int(min(max(nbytes, 16 * 1024 * 1024), V7X_VMEM_BYTES - 8 * 1024 * 1024))


def _nbytes(shape, dtype):
    return math.prod(shape) * jnp.dtype(dtype).itemsize


def _rms(x, g):
    ms = jnp.mean(x * x, axis=-1, keepdims=True)
    return x * lax.rsqrt(ms + EPS) * g


def _dot(a, b):
    return jnp.dot(a, b, preferred_element_type=F32)


def _sigmoid(x):
    return 1.0 / (1.0 + jnp.exp(-x))


def _in_proj_kernel(x_ref, g_ref, cos_ref, sin_ref, wm_ref, wk_ref,
                    z_ref, kpe_ref, kpeb_ref, u_scr):
    @pl.when(pl.program_id(1) == 0)
    def _():
        u = _rms(x_ref[...], g_ref[...]).astype(BF16)
        u_scr[...] = u
        kk = _dot(u, wk_ref[...])
        kpe = kk[:, :QK_ROPE] * cos_ref[...] + kk[:, QK_ROPE:] * sin_ref[...]
        kpe_ref[...] = kpe
        kpeb_ref[...] = kpe.astype(BF16)

    z_ref[...] = _dot(u_scr[...], wm_ref[...])


def _in_proj(x, g, cos64, sin64, w_main, w_kpe):
    n, d = x.shape
    nout = w_main.shape[1]
    tm = min(1024, n)
    tn = 1024
    assert n % tm == 0 and nout % tn == 0
    est = (2 * (_nbytes((tm, d), F32) + _nbytes((d, tn), BF16) + _nbytes((tm, tn), F32))
           + 2 * _nbytes((tm, d), BF16) + _nbytes((tm, d), F32))
    return pl.pallas_call(
        _in_proj_kernel,
        grid=(n // tm, nout // tn),
        in_specs=[
            pl.BlockSpec((tm, d), lambda i, j: (i, 0)),
            pl.BlockSpec((1, d), lambda i, j: (0, 0)),
            pl.BlockSpec((tm, QK_ROPE), lambda i, j: (i, 0)),
            pl.BlockSpec((tm, QK_ROPE), lambda i, j: (i, 0)),
            pl.BlockSpec((d, tn), lambda i, j: (0, j)),
            pl.BlockSpec((d, 2 * QK_ROPE), lambda i, j: (0, 0)),
        ],
        out_specs=[
            pl.BlockSpec((tm, tn), lambda i, j: (i, j)),
            pl.BlockSpec((tm, QK_ROPE), lambda i, j: (i, 0)),
            pl.BlockSpec((tm, QK_ROPE), lambda i, j: (i, 0)),
        ],
        out_shape=[
            jax.ShapeDtypeStruct((n, nout), F32),
            jax.ShapeDtypeStruct((n, QK_ROPE), F32),
            jax.ShapeDtypeStruct((n, QK_ROPE), BF16),
        ],
        scratch_shapes=[pltpu.VMEM((tm, d), BF16)],
        compiler_params=pltpu.CompilerParams(
            dimension_semantics=("parallel", "arbitrary"),
            vmem_limit_bytes=_vmem_limit(est)),
        name="in_proj",
    )(x, g, cos64, sin64, w_main, w_kpe)


def _mla_q_kernel(cq_ref, ckv_ref, cos_ref, sin_ref, gq_ref, gkv_ref,
                  wn_ref, wp_ref, wpr_ref, wuk_ref,
                  qabs_ref, qpe_ref, ckv_out_ref, ckvb_ref):
    cqn = _rms(cq_ref[...], gq_ref[...]).astype(BF16)
    qn = _dot(cqn, wn_ref[...]).astype(BF16)
    for h in range(MLA_HEADS):
        qa = _dot(qn[:, h * QK_NOPE:(h + 1) * QK_NOPE], wuk_ref[h])
        qabs_ref[:, h * KV_LORA:(h + 1) * KV_LORA] = (qa * ATTN_SCALE).astype(BF16)
    x = _dot(cqn, wp_ref[...])
    xr = _dot(cqn, wpr_ref[...])
    reps = MLA_HEADS * QK_ROPE // LANES
    c = jnp.concatenate([cos_ref[...]] * reps, axis=1)
    s = jnp.concatenate([sin_ref[...]] * reps, axis=1)
    qpe_ref[...] = ((x * c + xr * s) * ATTN_SCALE).astype(BF16)
    ckv = _rms(ckv_ref[...], gkv_ref[...])
    ckv_out_ref[...] = ckv
    ckvb_ref[...] = ckv.astype(BF16)


def _mla_q(z, cos128, sin128, g_q, g_kv, w_nope, w_pe, w_pe_rot, w_ukT):
    n = z.shape[0]
    tm = min(512, n)
    assert n % tm == 0
    hw = MLA_HEADS * KV_LORA
    pw = MLA_HEADS * QK_ROPE
    est = (2 * (2 * _nbytes((tm, Q_LORA), F32) + 2 * _nbytes((tm, LANES), F32)
                + _nbytes((tm, hw), BF16) + _nbytes((tm, pw), BF16)
                + _nbytes((tm, KV_LORA), F32) + _nbytes((tm, KV_LORA), BF16)
                + _nbytes((Q_LORA, MLA_HEADS * QK_NOPE), BF16) + 2 * _nbytes((Q_LORA, pw), BF16)
                + _nbytes(w_ukT.shape, BF16))
           + 4 * _nbytes((tm, MLA_HEADS * QK_NOPE), F32))
    full = lambda shape: pl.BlockSpec(shape, lambda i: (0,) * len(shape))
    return pl.pallas_call(
        _mla_q_kernel,
        grid=(n // tm,),
        in_specs=[
            pl.BlockSpec((tm, Q_LORA), lambda i: (i, 0)),
            pl.BlockSpec((tm, KV_LORA), lambda i: (i, 1)),
            pl.BlockSpec((tm, LANES), lambda i: (i, 0)),
            pl.BlockSpec((tm, LANES), lambda i: (i, 0)),
            full((1, Q_LORA)), full((1, KV_LORA)),
            full(w_nope.shape), full(w_pe.shape), full(w_pe_rot.shape), full(w_ukT.shape),
        ],
        out_specs=[
            pl.BlockSpec((tm, hw), lambda i: (i, 0)),
            pl.BlockSpec((tm, pw), lambda i: (i, 0)),
            pl.BlockSpec((tm, KV_LORA), lambda i: (i, 0)),
            pl.BlockSpec((tm, KV_LORA), lambda i: (i, 0)),
        ],
        out_shape=[
            jax.ShapeDtypeStruct((n, hw), BF16),
            jax.ShapeDtypeStruct((n, pw), BF16),
            jax.ShapeDtypeStruct((n, KV_LORA), F32),
            jax.ShapeDtypeStruct((n, KV_LORA), BF16),
        ],
        compiler_params=pltpu.CompilerParams(
            dimension_semantics=("parallel",), vmem_limit_bytes=_vmem_limit(est)),
        name="mla_q",
    )(z, z, cos128, sin128, g_q, g_kv, w_nope, w_pe, w_pe_rot, w_ukT)


def _softmax_update(s, v, m_scr, l_scr, acc_scr):
    m_prev = m_scr[...]
    m_new = jnp.maximum(m_prev, jnp.max(s, axis=1, keepdims=True))
    alpha = jnp.exp(m_prev - m_new)
    p = jnp.exp(s - pltpu.repeat(m_new, s.shape[1] // LANES, axis=1))
    l_scr[...] = alpha * l_scr[...] + jnp.sum(p, axis=1, keepdims=True)
    m_scr[...] = m_new
    acc_scr[...] = (acc_scr[...] * pltpu.repeat(alpha, acc_scr.shape[1] // LANES, axis=1)
                    + _dot(p.astype(BF16), v))


def _softmax_init(m_scr, l_scr, acc_scr):
    m_scr[...] = jnp.full(m_scr.shape, -jnp.inf, F32)
    l_scr[...] = jnp.zeros(l_scr.shape, F32)
    acc_scr[...] = jnp.zeros(acc_scr.shape, F32)


def _softmax_result(l_scr, acc_scr):
    return acc_scr[...] / pltpu.repeat(l_scr[...], acc_scr.shape[1] // LANES, axis=1)


def _attn_prompt_kernel(qa_ref, qp_ref, kc_ref, kr_ref, o_ref, m_scr, l_scr, acc_scr, *, tq, tk):
    i = pl.program_id(1)
    rows = tq * MLA_HEADS
    _softmax_init(m_scr, l_scr, acc_scr)

    def scores(start):
        kc = kc_ref[pl.ds(start, tk), :]
        kr = kr_ref[pl.ds(start, tk), :]
        s = lax.dot_general(qa_ref[...], kc, _NT, preferred_element_type=F32)
        s = s + lax.dot_general(qp_ref[...], kr, _NT, preferred_element_type=F32)
        return s, kc

    def full_chunk(c, carry):
        s, kc = scores(pl.multiple_of(c * tk, tk))
        _softmax_update(s, kc, m_scr, l_scr, acc_scr)
        return carry

    n_full = (i * tq) // tk
    lax.fori_loop(0, n_full, full_chunk, 0)

    start = pl.multiple_of(n_full * tk, tk)
    s, kc = scores(start)
    row_tok = i * tq + lax.shift_right_logical(lax.broadcasted_iota(jnp.int32, (rows, tk), 0), 3)
    col = start + lax.broadcasted_iota(jnp.int32, (rows, tk), 1)
    s = jnp.where(col <= row_tok, s, -jnp.inf)
    _softmax_update(s, kc, m_scr, l_scr, acc_scr)
    o_ref[...] = _softmax_result(l_scr, acc_scr).astype(o_ref.dtype)


def _attn_prompt(q_abs, q_pe, ckv_b, kpe_b):
    b, th, _ = q_abs.shape
    t = th // MLA_HEADS
    tq = min(128, t)
    tk = min(256, t)
    assert t % tk == 0 and tk % tq == 0 and MLA_HEADS == 8
    rows = tq * MLA_HEADS
    est = (2 * (_nbytes((rows, KV_LORA), BF16) * 2 + _nbytes((rows, LANES), BF16)
                + _nbytes((t, KV_LORA), BF16) + _nbytes((t, LANES), BF16))
           + 2 * _nbytes((rows, LANES), F32) + _nbytes((rows, KV_LORA), F32)
           + 6 * _nbytes((rows, tk), F32))
    return pl.pallas_call(
        functools.partial(_attn_prompt_kernel, tq=tq, tk=tk),
        grid=(b, t // tq),
        in_specs=[
            pl.BlockSpec((None, rows, KV_LORA), lambda bi, i: (bi, i, 0)),
            pl.BlockSpec((None, rows, QK_ROPE), lambda bi, i: (bi, i, 0)),
            pl.BlockSpec((None, t, KV_LORA), lambda bi, i: (bi, 0, 0)),
            pl.BlockSpec((None, t, QK_ROPE), lambda bi, i: (bi, 0, 0)),
        ],
        out_specs=pl.BlockSpec((None, rows, KV_LORA), lambda bi, i: (bi, i, 0)),
        out_shape=jax.ShapeDtypeStruct((b, th, KV_LORA), BF16),
        scratch_shapes=[pltpu.VMEM((rows, LANES), F32), pltpu.VMEM((rows, LANES), F32),
                        pltpu.VMEM((rows, KV_LORA), F32)],
        compiler_params=pltpu.CompilerParams(
            dimension_semantics=("parallel", "parallel"), vmem_limit_bytes=_vmem_limit(est)),
        name="attn_prompt",
    )(q_abs, q_pe, ckv_b, kpe_b)


def _attn_sample_kernel(pt_ref, qa_ref, qp_ref, cn_ref, kn_ref, *rest, pages):
    lat_refs = rest[:pages]
    rope_refs = rest[pages:2 * pages]
    o_ref, latb_scr, ropeb_scr, m_scr, l_scr, acc_scr = rest[2 * pages:]
    c = pl.program_id(1)
    rows, t_new = qa_ref.shape[0], cn_ref.shape[0]

    @pl.when(c == 0)
    def _():
        _softmax_init(m_scr, l_scr, acc_scr)

    for k in range(pages):
        latb_scr[k * PAGE_SIZE:(k + 1) * PAGE_SIZE, :] = lat_refs[k][...].astype(BF16)
        ropeb_scr[k * PAGE_SIZE:(k + 1) * PAGE_SIZE, :] = rope_refs[k][...].astype(BF16)
    s = lax.dot_general(qa_ref[...], latb_scr[...], _NT, preferred_element_type=F32)
    s = s + lax.dot_general(qp_ref[...], ropeb_scr[...], _NT, preferred_element_type=F32)
    _softmax_update(s, latb_scr[...], m_scr, l_scr, acc_scr)

    @pl.when(c == pl.num_programs(1) - 1)
    def _():
        pad = LANES - t_new
        cn = jnp.concatenate([cn_ref[...], jnp.zeros((pad, KV_LORA), F32)], axis=0).astype(BF16)
        kn = jnp.concatenate([kn_ref[...], jnp.zeros((pad, QK_ROPE), F32)], axis=0).astype(BF16)
        sn = lax.dot_general(qa_ref[...], cn, _NT, preferred_element_type=F32)
        sn = sn + lax.dot_general(qp_ref[...], kn, _NT, preferred_element_type=F32)
        row_tok = lax.shift_right_logical(lax.broadcasted_iota(jnp.int32, (rows, LANES), 0), 3)
        col = lax.broadcasted_iota(jnp.int32, (rows, LANES), 1)
        sn = jnp.where(col <= row_tok, sn, -jnp.inf)
        _softmax_update(sn, cn, m_scr, l_scr, acc_scr)
        o_ref[...] = _softmax_result(l_scr, acc_scr).astype(o_ref.dtype)


def _attn_sample(page_table, q_abs, q_pe, ckv_new, kpe_new, cache_lat, cache_rope):
    s_, rows, _ = q_abs.shape
    t_new = ckv_new.shape[1]
    n_pages = page_table.shape[1]
    pages = math.gcd(n_pages, 16)
    assert MLA_HEADS == 8 and rows == t_new * MLA_HEADS and t_new <= LANES
    pt_flat = page_table.reshape(-1)

    def page_spec(width, k):
        return pl.BlockSpec(
            (None, PAGE_SIZE, width),
            lambda b, c, pt: (pt[b * n_pages + c * pages + k], 0, 0))

    keys = pages * PAGE_SIZE
    est = (2 * pages * (_nbytes((PAGE_SIZE, KV_LORA), F32) + _nbytes((PAGE_SIZE, LANES), F32))
           + _nbytes((keys, KV_LORA), BF16) + _nbytes((keys, LANES), BF16)
           + 6 * _nbytes((rows, keys), F32) + 8 * _nbytes((rows, KV_LORA), F32))
    grid_spec = pltpu.PrefetchScalarGridSpec(
        num_scalar_prefetch=1,
        grid=(s_, n_pages // pages),
        in_specs=[
            pl.BlockSpec((None, rows, KV_LORA), lambda b, c, pt: (b, 0, 0)),
            pl.BlockSpec((None, rows, QK_ROPE), lambda b, c, pt: (b, 0, 0)),
            pl.BlockSpec((None, t_new, KV_LORA), lambda b, c, pt: (b, 0, 0)),
            pl.BlockSpec((None, t_new, QK_ROPE), lambda b, c, pt: (b, 0, 0)),
        ] + [page_spec(KV_LORA, k) for k in range(pages)]
          + [page_spec(QK_ROPE, k) for k in range(pages)],
        out_specs=pl.BlockSpec((None, rows, KV_LORA), lambda b, c, pt: (b, 0, 0)),
        scratch_shapes=[pltpu.VMEM((keys, KV_LORA), BF16), pltpu.VMEM((keys, QK_ROPE), BF16),
                        pltpu.VMEM((rows, LANES), F32), pltpu.VMEM((rows, LANES), F32),
                        pltpu.VMEM((rows, KV_LORA), F32)],
    )
    return pl.pallas_call(
        functools.partial(_attn_sample_kernel, pages=pages),
        grid_spec=grid_spec,
        out_shape=jax.ShapeDtypeStruct((s_, rows, KV_LORA), BF16),
        compiler_params=pltpu.CompilerParams(
            dimension_semantics=("parallel", "arbitrary"), vmem_limit_bytes=_vmem_limit(est)),
        name="attn_sample",
    )(pt_flat, q_abs, q_pe, ckv_new, kpe_new, *([cache_lat] * pages), *([cache_rope] * pages))


def _hgrn_chunk(hq, hf, v, hg, st, lb, g_out, tri):
    c = hq.shape[0]
    sub = min(HG_SUB, c)
    q = hq * _sigmoid(hq)
    log_f = jnp.log(lb + (1.0 - lb) * _sigmoid(hf))
    k = (1.0 - lb) * _sigmoid(-hf)
    b = jnp.dot(tri, log_f, preferred_element_type=F32, precision=lax.Precision.HIGHEST)
    vb = v.astype(BF16)
    stb = st.astype(BF16)
    o_parts = []
    row = lax.broadcasted_iota(jnp.int32, (sub, hq.shape[1]), 0)
    for i in range(c // sub):
        lo, hi = i * sub, (i + 1) * sub
        bi, qi, ki, vi = b[lo:hi], q[lo:hi], k[lo:hi], v[lo:hi]
        o_i = jnp.zeros((sub, v.shape[1]), F32)
        for s in range(sub):
            e = jnp.where(row >= s, jnp.exp(bi - bi[s:s + 1]), 0.0)
            a_col = jnp.sum(qi * ki[s:s + 1] * e, axis=1, keepdims=True)
            o_i = o_i + a_col * vi[s:s + 1]
        if i > 0:
            r = b[lo - 1:lo]
            qh = (qi * jnp.exp(bi - r)).astype(BF16)
            kh = (k[:lo] * jnp.exp(r - b[:lo])).astype(BF16)
            a_off = lax.dot_general(qh, kh, _NT, preferred_element_type=F32)
            o_i = o_i + _dot(a_off.astype(BF16), vb[:lo])
        o_parts.append(o_i)
    o = o_parts[0] if len(o_parts) == 1 else jnp.concatenate(o_parts, axis=0)
    o = o + lax.dot_general((q * jnp.exp(b)).astype(BF16), stb, _NT, preferred_element_type=F32)
    b_last = b[c - 1:c]
    kd = (k * jnp.exp(b_last - b)).astype(BF16)
    st_new = st * jnp.exp(b_last) + lax.dot_general(vb, kd, _TN, preferred_element_type=F32)
    o = _rms(o, g_out) * (hg * _sigmoid(hg))
    return o, st_new


def _hgrn_lb(lb_ref):
    a = lb_ref[...]
    a0, a1 = a[0:1], a[1:2]
    m = jnp.maximum(a0, a1)
    e0, e1 = jnp.exp(a0 - m), jnp.exp(a1 - m)
    return e0 / (e0 + e1)


def _tri(c):
    return (lax.broadcasted_iota(jnp.int32, (c, c), 0)
            >= lax.broadcasted_iota(jnp.int32, (c, c), 1)).astype(F32)


def _hgrn_prompt_kernel(hq_ref, hf_ref, hi_ref, hg_ref, lb_ref, g_ref, o_ref, s_ref, st_scr,
                        *, chunk, n_chunks):
    step = pl.program_id(2)

    @pl.when(step == 0)
    def _():
        st_scr[...] = jnp.zeros(st_scr.shape, F32)

    lb = _hgrn_lb(lb_ref)
    g = g_ref[...]
    tri = _tri(chunk)

    def body(ci, carry):
        rows = pl.ds(pl.multiple_of(ci * chunk, chunk), chunk)
        o, st_new = _hgrn_chunk(hq_ref[rows, :], hf_ref[rows, :], hi_ref[rows, :], hg_ref[rows, :],
                                st_scr[...], lb, g, tri)
        o_ref[rows, :] = o.astype(o_ref.dtype)
        st_scr[...] = st_new
        return carry

    lax.fori_loop(0, n_chunks, body, 0)

    @pl.when(step == pl.num_programs(2) - 1)
    def _():
        s_ref[...] = st_scr[...].T


def _hgrn_prompt(z, hg_lb, g_out, b, t):
    chunk = math.gcd(t, HG_CHUNK)
    n_chunks = min(4, t // chunk)
    tb = chunk * n_chunks
    assert t % tb == 0
    steps = t // tb
    col0 = (Q_LORA + KV_LORA) // HG_DK

    def zspec(group):
        return pl.BlockSpec((tb, HG_DK),
                            lambda bi, h, s: (bi * steps + s, col0 + group * HG_HEADS + h))

    est = 2 * (4 * _nbytes((tb, HG_DK), F32) + _nbytes((tb, HG_DV), BF16)) + 64 * _nbytes((HG_DK, HG_DV), F32)
    return pl.pallas_call(
        functools.partial(_hgrn_prompt_kernel, chunk=chunk, n_chunks=n_chunks),
        grid=(b, HG_HEADS, steps),
        in_specs=[zspec(0), zspec(1), zspec(2), zspec(3),
                  pl.BlockSpec((2, HG_DK), lambda bi, h, s: (0, h)),
                  pl.BlockSpec((1, HG_DV), lambda bi, h, s: (0, 0))],
        out_specs=[pl.BlockSpec((tb, HG_DV), lambda bi, h, s: (bi * steps + s, h)),
                   pl.BlockSpec((None, None, HG_DK, HG_DV), lambda bi, h, s: (bi, h, 0, 0))],
        out_shape=[jax.ShapeDtypeStruct((b * t, HG_HEADS * HG_DV), BF16),
                   jax.ShapeDtypeStruct((b, HG_HEADS, HG_DK, HG_DV), F32)],
        scratch_shapes=[pltpu.VMEM((HG_DV, HG_DK), F32)],
        compiler_params=pltpu.CompilerParams(
            dimension_semantics=("parallel", "parallel", "arbitrary"),
            vmem_limit_bytes=_vmem_limit(est)),
        name="hgrn_prompt",
    )(z, z, z, z, hg_lb, g_out)


def _hgrn_sample_kernel(hq_ref, hf_ref, hi_ref, hg_ref, s0_ref, lb_ref, g_ref, o_ref, s_ref):
    lb_all = _hgrn_lb(lb_ref)
    g = g_ref[...]
    tri = _tri(hq_ref.shape[0])
    for h in range(HG_HEADS):
        cols = slice(h * HG_DK, (h + 1) * HG_DK)
        o, st_new = _hgrn_chunk(hq_ref[:, cols], hf_ref[:, cols], hi_ref[:, cols], hg_ref[:, cols],
                                s0_ref[h].T, lb_all[:, cols], g, tri)
        o_ref[:, cols] = o
        s_ref[h] = st_new.T


def _hgrn_sample(z, state, hg_lb, g_out, t):
    nseq = state.shape[0]
    assert HG_CHUNK % t == 0 and t % 8 == 0
    w = HG_HEADS * HG_DK
    col0 = (Q_LORA + KV_LORA) // w
    zspec = lambda group: pl.BlockSpec((t, w), lambda bi: (bi, col0 + group))
    sspec = pl.BlockSpec((None, HG_HEADS, HG_DK, HG_DV), lambda bi: (bi, 0, 0, 0))
    est = 2 * (5 * _nbytes((t, w), F32) + 2 * _nbytes((HG_HEADS, HG_DK, HG_DV), F32)) + 64 * _nbytes((HG_DK, HG_DV), F32)
    return pl.pallas_call(
        _hgrn_sample_kernel,
        grid=(nseq,),
        in_specs=[zspec(0), zspec(1), zspec(2), zspec(3), sspec,
                  pl.BlockSpec((2, w), lambda bi: (0, 0)),
                  pl.BlockSpec((1, HG_DV), lambda bi: (0, 0))],
        out_specs=[pl.BlockSpec((t, w), lambda bi: (bi, 0)), sspec],
        out_shape=[jax.ShapeDtypeStruct((nseq * t, w), F32),
                   jax.ShapeDtypeStruct(state.shape, F32)],
        compiler_params=pltpu.CompilerParams(
            dimension_semantics=("parallel",), vmem_limit_bytes=_vmem_limit(est)),
        name="hgrn_sample",
    )(z, z, z, z, state, hg_lb, g_out)


def _out_proj_kernel(x_ref, olat_ref, ohg_ref, wuv_ref, gm_ref, wo_ref, h_ref):
    parts = []
    for h in range(MLA_HEADS):
        om = _dot(olat_ref[:, h * KV_LORA:(h + 1) * KV_LORA], wuv_ref[h])
        parts.append(_rms(om, gm_ref[...]).astype(BF16))
    o_mla = jnp.concatenate(parts, axis=1)
    w = MLA_HEADS * V_HEAD
    h_ref[...] = (x_ref[...] + _dot(o_mla, wo_ref[:w, :])
                  + _dot(ohg_ref[...].astype(BF16), wo_ref[w:, :]))


def _out_proj(x, o_lat, o_hg, w_uvT, g_mla, w_o):
    n, d = x.shape
    tm = min(256, n)
    assert n % tm == 0
    lw = MLA_HEADS * KV_LORA
    hw = o_hg.shape[1]
    est = (2 * (2 * _nbytes((tm, d), F32) + _nbytes((tm, lw), BF16) + _nbytes((tm, hw), o_hg.dtype)
                + _nbytes(w_uvT.shape, BF16) + _nbytes(w_o.shape, BF16))
           + 4 * _nbytes((tm, d), F32))
    full = lambda shape: pl.BlockSpec(shape, lambda i: (0,) * len(shape))
    return pl.pallas_call(
        _out_proj_kernel,
        grid=(n // tm,),
        in_specs=[pl.BlockSpec((tm, d), lambda i: (i, 0)),
                  pl.BlockSpec((tm, lw), lambda i: (i, 0)),
                  pl.BlockSpec((tm, hw), lambda i: (i, 0)),
                  full(w_uvT.shape), full((1, V_HEAD)), full(w_o.shape)],
        out_specs=pl.BlockSpec((tm, d), lambda i: (i, 0)),
        out_shape=jax.ShapeDtypeStruct((n, d), F32),
        compiler_params=pltpu.CompilerParams(
            dimension_semantics=("parallel",), vmem_limit_bytes=_vmem_limit(est)),
        name="out_proj",
    )(x, o_lat, o_hg, w_uvT, g_mla, w_o)


def _ffn_kernel(h_ref, g_ref, wg_ref, wu_ref, wd_ref, o_ref, hn_scr):
    @pl.when(pl.program_id(1) == 0)
    def _():
        h = h_ref[...]
        hn_scr[...] = _rms(h, g_ref[...]).astype(BF16)
        o_ref[...] = h

    hn = hn_scr[...]
    gate = _dot(hn, wg_ref[...])
    up = _dot(hn, wu_ref[...])
    act = (gate * _sigmoid(gate) * up).astype(BF16)
    o_ref[...] += _dot(act, wd_ref[...])


def _ffn(h, g, w_gate, w_up, w_down):
    n, d = h.shape
    dff = w_gate.shape[1]
    tm = min(512, n)
    tf = 512
    assert n % tm == 0 and dff % tf == 0
    est = (2 * (2 * _nbytes((tm, d), F32) + 3 * _nbytes((d, tf), BF16))
           + _nbytes((tm, d), BF16) + 4 * _nbytes((tm, tf), F32) + _nbytes((tm, d), F32))
    return pl.pallas_call(
        _ffn_kernel,
        grid=(n // tm, dff // tf),
        in_specs=[pl.BlockSpec((tm, d), lambda i, j: (i, 0)),
                  pl.BlockSpec((1, d), lambda i, j: (0, 0)),
                  pl.BlockSpec((d, tf), lambda i, j: (0, j)),
                  pl.BlockSpec((d, tf), lambda i, j: (0, j)),
                  pl.BlockSpec((tf, d), lambda i, j: (j, 0))],
        out_specs=pl.BlockSpec((tm, d), lambda i, j: (i, 0)),
        out_shape=jax.ShapeDtypeStruct((n, d), F32),
        scratch_shapes=[pltpu.VMEM((tm, d), BF16)],
        compiler_params=pltpu.CompilerParams(
            dimension_semantics=("parallel", "arbitrary"), vmem_limit_bytes=_vmem_limit(est)),
        name="ffn",
    )(h, g, w_gate, w_up, w_down)


def _ple_final_kernel(h_ref, p_ref, gp_ref, wg_ref, wp_ref, gf_ref, y_ref):
    h = h_ref[...]
    a = _dot(_rms(h, gp_ref[...]).astype(BF16), wg_ref[...])
    pp = _dot(p_ref[...].astype(BF16), wp_ref[...])
    y_ref[...] = _rms(h + pp * _sigmoid(a), gf_ref[...])


def _ple_final(h, p, g_ple, w_gate, w_proj, g_final):
    n, d = h.shape
    pd = p.shape[1]
    tm = min(256, n)
    assert n % tm == 0
    est = (2 * (2 * _nbytes((tm, d), F32) + _nbytes((tm, pd), F32)
                + _nbytes(w_gate.shape, BF16) + _nbytes(w_proj.shape, BF16))
           + 4 * _nbytes((tm, d), F32))
    full = lambda shape: pl.BlockSpec(shape, lambda i: (0,) * len(shape))
    return pl.pallas_call(
        _ple_final_kernel,
        grid=(n // tm,),
        in_specs=[pl.BlockSpec((tm, d), lambda i: (i, 0)),
                  pl.BlockSpec((tm, pd), lambda i: (i, 0)),
                  full((1, d)), full(w_gate.shape), full(w_proj.shape), full((1, d))],
        out_specs=pl.BlockSpec((tm, d), lambda i: (i, 0)),
        out_shape=jax.ShapeDtypeStruct((n, d), F32),
        compiler_params=pltpu.CompilerParams(
            dimension_semantics=("parallel",), vmem_limit_bytes=_vmem_limit(est)),
        name="ple_final",
    )(h, p, g_ple, w_gate, w_proj, g_final)


def _rope_tables(pos, reps):
    inv = ROPE_BASE ** (-jnp.arange(0, QK_ROPE, 2, dtype=F32) / QK_ROPE)
    ang = pos.astype(F32)[:, None] * inv[None, :]
    cos, sin = jnp.cos(ang), jnp.sin(ang)
    cos64 = jnp.concatenate([cos, cos], axis=1)
    sin64 = jnp.concatenate([-sin, sin], axis=1)
    cos64, sin64 = jnp.tile(cos64, (reps, 1)), jnp.tile(sin64, (reps, 1))
    return cos64, sin64, jnp.tile(cos64, (1, 2)), jnp.tile(sin64, (1, 2))


def _prep_weights(g_mix, w_in, g_q, w_uq, g_kv, w_uk, w_uv, g_mla_out, g_hg_out, w_o, g_ffn,
                  w_ffn_gate, w_ffn_up, w_ffn_down, g_ple, w_ple_gate, w_ple_proj, g_final):
    row = lambda g: g.reshape(1, -1).astype(F32)
    lo = Q_LORA + KV_LORA
    w_main = jnp.concatenate([w_in[:, :lo], w_in[:, lo + QK_ROPE:]], axis=1).astype(BF16)
    w_k = w_in[:, lo:lo + QK_ROPE]
    w_kpe = jnp.concatenate([w_k, jnp.roll(w_k, QK_ROPE // 2, axis=1)], axis=1).astype(BF16)
    w_nope = w_uq[:, :, :QK_NOPE].reshape(Q_LORA, MLA_HEADS * QK_NOPE).astype(BF16)
    pe = w_uq[:, :, QK_NOPE:]
    w_pe = pe.reshape(Q_LORA, MLA_HEADS * QK_ROPE).astype(BF16)
    w_pe_rot = jnp.roll(pe, QK_ROPE // 2, axis=2).reshape(Q_LORA, MLA_HEADS * QK_ROPE).astype(BF16)
    w_ukT = jnp.transpose(w_uk, (1, 2, 0)).astype(BF16)
    w_uvT = jnp.transpose(w_uv, (1, 0, 2)).astype(BF16)
    return dict(
        g_mix=row(g_mix), w_main=w_main, w_kpe=w_kpe, g_q=row(g_q), g_kv=row(g_kv),
        w_nope=w_nope, w_pe=w_pe, w_pe_rot=w_pe_rot, w_ukT=w_ukT, w_uvT=w_uvT,
        g_mla=row(g_mla_out), g_hg=row(g_hg_out), w_o=w_o.astype(BF16), g_ffn=row(g_ffn),
        w_gate=w_ffn_gate.astype(BF16), w_up=w_ffn_up.astype(BF16), w_down=w_ffn_down.astype(BF16),
        g_ple=row(g_ple), w_ple_gate=w_ple_gate.astype(BF16), w_ple_proj=w_ple_proj.astype(BF16),
        g_final=row(g_final))


def _front(x, pos, w):
    b, t, d = x.shape
    n = b * t
    cos64, sin64, cos128, sin128 = _rope_tables(pos, b)
    x2 = x.reshape(n, d)
    z, kpe, kpe_b = _in_proj(x2, w["g_mix"], cos64, sin64, w["w_main"], w["w_kpe"])
    q_abs, q_pe, ckv, ckv_b = _mla_q(z, cos128, sin128, w["g_q"], w["g_kv"],
                                     w["w_nope"], w["w_pe"], w["w_pe_rot"], w["w_ukT"])
    q_abs = q_abs.reshape(b, t * MLA_HEADS, KV_LORA)
    q_pe = q_pe.reshape(b, t * MLA_HEADS, QK_ROPE)
    return x2, z, q_abs, q_pe, ckv, ckv_b, kpe, kpe_b


def _back(x2, o_lat, o_hg, p, w):
    n = x2.shape[0]
    h = _out_proj(x2, o_lat.reshape(n, MLA_HEADS * KV_LORA), o_hg, w["w_uvT"], w["g_mla"], w["w_o"])
    h = _ffn(h, w["g_ffn"], w["w_gate"], w["w_up"], w["w_down"])
    return _ple_final(h, p.reshape(n, -1), w["g_ple"], w["w_ple_gate"], w["w_ple_proj"], w["g_final"])


def kernel(x_prompt, x_sample, cache_kv_latent, cache_k_rope, state_hgrn, page_table, p_prompt, p_sample, g_mix, w_in, g_q, w_uq, g_kv, w_uk, w_uv, g_mla_out, hg_lb, g_hg_out, w_o, g_ffn, w_ffn_gate, w_ffn_up, w_ffn_down, g_ple, w_ple_gate, w_ple_proj, g_final):
    depth = g_mix.shape[0]
    assert depth == 1 and hg_lb.shape[0] == 2, "single-layer trunk only"
    w = _prep_weights(g_mix[0], w_in[0], g_q[0], w_uq[0], g_kv[0], w_uk[0], w_uv[0], g_mla_out[0],
                      g_hg_out[0], w_o[0], g_ffn[0], w_ffn_gate[0], w_ffn_up[0], w_ffn_down[0],
                      g_ple[0], w_ple_gate[0], w_ple_proj[0], g_final)
    hg_lb = hg_lb.astype(F32)
    bp, tp, d = x_prompt.shape
    bs, ts, _ = x_sample.shape
    past_len = page_table.shape[1] * PAGE_SIZE

    x2, z, q_abs, q_pe, ckv, ckv_b, kpe, kpe_b = _front(x_prompt, jnp.arange(tp, dtype=jnp.int32), w)
    o_lat = _attn_prompt(q_abs, q_pe, ckv_b.reshape(bp, tp, KV_LORA), kpe_b.reshape(bp, tp, QK_ROPE))
    o_hg, st_p = _hgrn_prompt(z, hg_lb, w["g_hg"], bp, tp)
    y_prompt = _back(x2, o_lat, o_hg, p_prompt.reshape(p_prompt.shape[1:]), w).reshape(bp, tp, d)
    lat_p = ckv.reshape(1, bp, tp, KV_LORA)
    rope_p = kpe.reshape(1, bp, tp, QK_ROPE)

    pos_s = past_len + jnp.arange(ts, dtype=jnp.int32)
    x2, z, q_abs, q_pe, ckv, ckv_b, kpe, kpe_b = _front(x_sample, pos_s, w)
    o_lat = _attn_sample(page_table, q_abs, q_pe, ckv.reshape(bs, ts, KV_LORA),
                         kpe.reshape(bs, ts, QK_ROPE),
                         cache_kv_latent.reshape(cache_kv_latent.shape[1:]),
                         cache_k_rope.reshape(cache_k_rope.shape[1:]))
    o_hg, st_s = _hgrn_sample(z, state_hgrn.reshape(state_hgrn.shape[1:]), hg_lb, w["g_hg"], ts)
    y_sample = _back(x2, o_lat, o_hg, p_sample.reshape(p_sample.shape[1:]), w).reshape(bs, ts, d)
    lat_s = ckv.reshape(1, bs, ts, KV_LORA)
    rope_s = kpe.reshape(1, bs, ts, QK_ROPE)

    return (y_prompt, y_sample, lat_p, rope_p, st_p[None], lat_s, rope_s, st_s[None])
```

```python
import functools
import math

import jax
import jax.numpy as jnp
from jax import lax
from jax.experimental import pallas as pl
from jax.experimental.pallas import tpu as pltpu

F32 = jnp.float32
BF16 = jnp.bfloat16

EPS = 1e-6
ROPE_BASE = 10000.0
PAGE_SIZE = 128
MLA_HEADS = 8
V_HEAD = 128
QK_NOPE = 128
QK_ROPE = 64
Q_LORA = 512
KV_LORA = 512
ATTN_SCALE = (QK_NOPE + QK_ROPE) ** -0.5
HG_HEADS = 8
HG_DK = 128
HG_DV = 128
HG_CHUNK = 64
HG_SUB = 16

V7X_VMEM_BYTES = 64 * 1024 * 1024
LANES = 128
SUBLANES = 8

QK_CAT = KV_LORA + LANES

_NT = (((1,), (1,)), ((), ()))
_TN = (((0,), (0,)), ((), ()))


def _vmem_limit(nbytes):
    return int(min(max(nbytes, 16 * 1024 * 1024), V7X_VMEM_BYTES - 8 * 1024 * 1024))


def _nbytes(shape, dtype):
    return math.prod(shape) * jnp.dtype(dtype).itemsize


def _rms(x, g):
    ms = jnp.mean(x * x, axis=-1, keepdims=True)
    return x * lax.rsqrt(ms + EPS) * g


def _dot(a, b):
    return jnp.dot(a, b, preferred_element_type=F32)


def _sigmoid(x):
    return 1.0 / (1.0 + jnp.exp(-x))


def _in_proj_kernel(x_ref, g_ref, cos_ref, sin_ref, wm_ref, wk_ref,
                    z_ref, kpe_ref, kpeb_ref, u_scr):
    @pl.when(pl.program_id(1) == 0)
    def _():
        u = _rms(x_ref[...], g_ref[...]).astype(BF16)
        u_scr[...] = u
        kk = _dot(u, wk_ref[...])
        kpe = kk[:, :LANES] * cos_ref[...] + kk[:, LANES:] * sin_ref[...]
        kpe_ref[...] = kpe[:, :QK_ROPE]
        kpeb_ref[...] = kpe.astype(BF16)

    z_ref[...] = _dot(u_scr[...], wm_ref[...])


def _in_proj(x, g, cos128, sin128, w_main, w_kpe):
    n, d = x.shape
    nout = w_main.shape[1]
    tm = min(1024, n)
    tn = 1024
    assert n % tm == 0 and nout % tn == 0
    est = (2 * (_nbytes((tm, d), F32) + _nbytes((d, tn), BF16) + _nbytes((tm, tn), F32))
           + 2 * _nbytes((tm, d), BF16) + _nbytes((tm, d), F32))
    return pl.pallas_call(
        _in_proj_kernel,
        grid=(n // tm, nout // tn),
        in_specs=[
            pl.BlockSpec((tm, d), lambda i, j: (i, 0)),
            pl.BlockSpec((1, d), lambda i, j: (0, 0)),
            pl.BlockSpec((tm, LANES), lambda i, j: (i, 0)),
            pl.BlockSpec((tm, LANES), lambda i, j: (i, 0)),
            pl.BlockSpec((d, tn), lambda i, j: (0, j)),
            pl.BlockSpec((d, 2 * LANES), lambda i, j: (0, 0)),
        ],
        out_specs=[
            pl.BlockSpec((tm, tn), lambda i, j: (i, j)),
            pl.BlockSpec((tm, QK_ROPE), lambda i, j: (i, 0)),
            pl.BlockSpec((tm, LANES), lambda i, j: (i, 0)),
        ],
        out_shape=[
            jax.ShapeDtypeStruct((n, nout), F32),
            jax.ShapeDtypeStruct((n, QK_ROPE), F32),
            jax.ShapeDtypeStruct((n, LANES), BF16),
        ],
        scratch_shapes=[pltpu.VMEM((tm, d), BF16)],
        compiler_params=pltpu.CompilerParams(
            dimension_semantics=("parallel", "arbitrary"),
            vmem_limit_bytes=_vmem_limit(est)),
        name="in_proj",
    )(x, g, cos128, sin128, w_main, w_kpe)


def _mla_q_kernel(cq_ref, ckv_ref, kpeb_ref, cos_ref, sin_ref, gq_ref, gkv_ref,
                  wn_ref, wp_ref, wpr_ref, wuk_ref,
                  qcat_ref, kcat_ref, ckv_out_ref):
    cqn = _rms(cq_ref[...], gq_ref[...]).astype(BF16)
    qn = _dot(cqn, wn_ref[...]).astype(BF16)
    x = _dot(cqn, wp_ref[...])
    xr = _dot(cqn, wpr_ref[...])
    c, s = cos_ref[...], sin_ref[...]
    for h in range(MLA_HEADS):
        col = h * QK_CAT
        qa = _dot(qn[:, h * QK_NOPE:(h + 1) * QK_NOPE], wuk_ref[h])
        qcat_ref[:, col:col + KV_LORA] = (qa * ATTN_SCALE).astype(BF16)
        lanes = slice(h * LANES, (h + 1) * LANES)
        qcat_ref[:, col + KV_LORA:col + QK_CAT] = (
            (x[:, lanes] * c + xr[:, lanes] * s) * ATTN_SCALE).astype(BF16)
    ckv = _rms(ckv_ref[...], gkv_ref[...])
    ckv_out_ref[...] = ckv
    kcat_ref[:, :KV_LORA] = ckv.astype(BF16)
    kcat_ref[:, KV_LORA:] = kpeb_ref[...]


def _mla_q(z, kpe_b, cos128, sin128, g_q, g_kv, w_nope, w_pe, w_pe_rot, w_ukT):
    n = z.shape[0]
    tm = min(512, n)
    assert n % tm == 0
    qw = MLA_HEADS * QK_CAT
    est = (2 * (2 * _nbytes((tm, Q_LORA), F32) + 2 * _nbytes((tm, LANES), F32) + _nbytes((tm, LANES), BF16)
                + _nbytes((tm, qw), BF16) + _nbytes((tm, QK_CAT), BF16) + _nbytes((tm, KV_LORA), F32)
                + 3 * _nbytes(w_nope.shape, BF16) + _nbytes(w_ukT.shape, BF16))
           + 6 * _nbytes((tm, MLA_HEADS * QK_NOPE), F32))
    full = lambda shape: pl.BlockSpec(shape, lambda i: (0,) * len(shape))
    return pl.pallas_call(
        _mla_q_kernel,
        grid=(n // tm,),
        in_specs=[
            pl.BlockSpec((tm, Q_LORA), lambda i: (i, 0)),
            pl.BlockSpec((tm, KV_LORA), lambda i: (i, 1)),
            pl.BlockSpec((tm, LANES), lambda i: (i, 0)),
            pl.BlockSpec((tm, LANES), lambda i: (i, 0)),
            pl.BlockSpec((tm, LANES), lambda i: (i, 0)),
            full((1, Q_LORA)), full((1, KV_LORA)),
            full(w_nope.shape), full(w_pe.shape), full(w_pe_rot.shape), full(w_ukT.shape),
        ],
        out_specs=[
            pl.BlockSpec((tm, qw), lambda i: (i, 0)),
            pl.BlockSpec((tm, QK_CAT), lambda i: (i, 0)),
            pl.BlockSpec((tm, KV_LORA), lambda i: (i, 0)),
        ],
        out_shape=[
            jax.ShapeDtypeStruct((n, qw), BF16),
            jax.ShapeDtypeStruct((n, QK_CAT), BF16),
            jax.ShapeDtypeStruct((n, KV_LORA), F32),
        ],
        compiler_params=pltpu.CompilerParams(
            dimension_semantics=("parallel",), vmem_limit_bytes=_vmem_limit(est)),
        name="mla_q",
    )(z, z, kpe_b, cos128, sin128, g_q, g_kv, w_nope, w_pe, w_pe_rot, w_ukT)


def _lane_tile(x, width):
    reps = width // LANES
    return x if reps == 1 else jnp.concatenate([x] * reps, axis=1)


def _softmax_update(s, v, m_scr, l_scr, acc_scr):
    m_prev = m_scr[...]
    m_new = jnp.maximum(m_prev, jnp.max(s, axis=1, keepdims=True))
    alpha = jnp.exp(m_prev - m_new)
    p = jnp.exp(s - _lane_tile(m_new, s.shape[1]))
    l_scr[...] = alpha * l_scr[...] + jnp.sum(p, axis=1, keepdims=True)
    m_scr[...] = m_new
    acc_scr[...] = (acc_scr[...] * _lane_tile(alpha, acc_scr.shape[1])
                    + _dot(p.astype(BF16), v))


def _softmax_init(m_scr, l_scr, acc_scr):
    m_scr[...] = jnp.full(m_scr.shape, -jnp.inf, F32)
    l_scr[...] = jnp.zeros(l_scr.shape, F32)
    acc_scr[...] = jnp.zeros(acc_scr.shape, F32)


def _softmax_result(l_scr, acc_scr):
    return acc_scr[...] / _lane_tile(l_scr[...], acc_scr.shape[1])


def _attn_prompt_kernel(q_ref, k_ref, o_ref, qs_scr, m_scr, l_scr, acc_scr, *, tq, tk):
    i = pl.program_id(1)
    rows = tq * MLA_HEADS
    for h in range(MLA_HEADS):
        qs_scr[h * tq:(h + 1) * tq, :] = q_ref[:, h * QK_CAT:(h + 1) * QK_CAT]
    _softmax_init(m_scr, l_scr, acc_scr)

    def scores(start):
        return lax.dot_general(qs_scr[...], k_ref[pl.ds(start, tk), :], _NT,
                               preferred_element_type=F32)

    def full_chunk(c, carry):
        start = pl.multiple_of(c * tk, tk)
        _softmax_update(scores(start), k_ref[pl.ds(start, tk), :KV_LORA], m_scr, l_scr, acc_scr)
        return carry

    n_full = (i * tq) // tk
    lax.fori_loop(0, n_full, full_chunk, 0)

    start = pl.multiple_of(n_full * tk, tk)
    row_tok = i * tq + jnp.bitwise_and(lax.broadcasted_iota(jnp.int32, (rows, tk), 0), tq - 1)
    col = start + lax.broadcasted_iota(jnp.int32, (rows, tk), 1)
    s = jnp.where(col <= row_tok, scores(start), -jnp.inf)
    _softmax_update(s, k_ref[pl.ds(start, tk), :KV_LORA], m_scr, l_scr, acc_scr)
    res = _softmax_result(l_scr, acc_scr).astype(o_ref.dtype)
    for h in range(MLA_HEADS):
        o_ref[:, h * KV_LORA:(h + 1) * KV_LORA] = res[h * tq:(h + 1) * tq, :]


def _attn_prompt(q_cat, k_cat, b, t):
    tq = min(128, t)
    tk = min(512, t)
    assert t % tk == 0 and tk % tq == 0 and tq & (tq - 1) == 0
    rows = tq * MLA_HEADS
    qw, ow = MLA_HEADS * QK_CAT, MLA_HEADS * KV_LORA
    est = (2 * (_nbytes((tq, qw), BF16) + _nbytes((t, QK_CAT), BF16) + _nbytes((tq, ow), BF16))
           + _nbytes((rows, QK_CAT), BF16) + 2 * _nbytes((rows, LANES), F32)
           + _nbytes((rows, KV_LORA), F32) + 6 * _nbytes((rows, tk), F32))
    nq = t // tq
    return pl.pallas_call(
        functools.partial(_attn_prompt_kernel, tq=tq, tk=tk),
        grid=(b, nq),
        in_specs=[
            pl.BlockSpec((tq, qw), lambda bi, i: (bi * nq + i, 0)),
            pl.BlockSpec((t, QK_CAT), lambda bi, i: (bi, 0)),
        ],
        out_specs=pl.BlockSpec((tq, ow), lambda bi, i: (bi * nq + i, 0)),
        out_shape=jax.ShapeDtypeStruct((b * t, ow), BF16),
        scratch_shapes=[pltpu.VMEM((rows, QK_CAT), BF16),
                        pltpu.VMEM((rows, LANES), F32), pltpu.VMEM((rows, LANES), F32),
                        pltpu.VMEM((rows, KV_LORA), F32)],
        compiler_params=pltpu.CompilerParams(
            dimension_semantics=("parallel", "parallel"), vmem_limit_bytes=_vmem_limit(est)),
        name="attn_prompt",
    )(q_cat, k_cat)


def _attn_sample_kernel(pt_ref, q_ref, cn_ref, kn_ref, lat_hbm, rope_hbm, o_ref,
                        lat_buf, rope_buf, sems, latb_scr, ropeb_scr, m_scr, l_scr, acc_scr,
                        *, pages, group, n_chunks):
    b = pl.program_id(0)
    total = pl.num_programs(0) * n_chunks
    rows, t_new = q_ref.shape[0], cn_ref.shape[0]

    def page_copies(first_page, slot, k):
        page = pt_ref[first_page + k]
        return (pltpu.make_async_copy(lat_hbm.at[page], lat_buf.at[slot, k], sems.at[0, slot]),
                pltpu.make_async_copy(rope_hbm.at[page], rope_buf.at[slot, k], sems.at[1, slot]))

    def start_chunk(g, slot):
        for k in range(pages):
            for cp in page_copies(g * pages, slot, k):
                cp.start()

    def wait_chunk(g, slot):
        for k in range(pages):
            for cp in page_copies(g * pages, slot, k):
                cp.wait()

    @pl.when(b == 0)
    def _():
        start_chunk(0, 0)

    _softmax_init(m_scr, l_scr, acc_scr)
    qa = q_ref[:, :KV_LORA]
    qp = q_ref[:, KV_LORA:KV_LORA + QK_ROPE]

    def chunk(c, carry):
        g = b * n_chunks + c
        slot = lax.rem(g, 2)

        @pl.when(g + 1 < total)
        def _():
            start_chunk(g + 1, 1 - slot)

        wait_chunk(g, slot)
        for j in range(pages // group):
            for k in range(j * group, (j + 1) * group):
                latb_scr[k * PAGE_SIZE:(k + 1) * PAGE_SIZE, :] = lat_buf[slot, k].astype(BF16)
                ropeb_scr[:, k * PAGE_SIZE:(k + 1) * PAGE_SIZE] = rope_buf[slot, k].astype(BF16)
            keys = slice(j * group * PAGE_SIZE, (j + 1) * group * PAGE_SIZE)
            s = lax.dot_general(qa, latb_scr[keys, :], _NT, preferred_element_type=F32)
            s = s + _dot(qp, ropeb_scr[:, keys])
            _softmax_update(s, latb_scr[keys, :], m_scr, l_scr, acc_scr)
        return carry

    lax.fori_loop(0, n_chunks, chunk, 0)

    pad = LANES - t_new
    cn = jnp.concatenate([cn_ref[...], jnp.zeros((pad, KV_LORA), F32)], axis=0).astype(BF16)
    kn = jnp.concatenate([kn_ref[...], jnp.zeros((pad, QK_ROPE), F32)], axis=0).astype(BF16)
    sn = lax.dot_general(qa, cn, _NT, preferred_element_type=F32)
    sn = sn + lax.dot_general(qp, kn, _NT, preferred_element_type=F32)
    row_tok = lax.shift_right_logical(lax.broadcasted_iota(jnp.int32, (rows, LANES), 0), 3)
    col = lax.broadcasted_iota(jnp.int32, (rows, LANES), 1)
    sn = jnp.where(col <= row_tok, sn, -jnp.inf)
    _softmax_update(sn, cn, m_scr, l_scr, acc_scr)
    o_ref[...] = _softmax_result(l_scr, acc_scr).astype(o_ref.dtype)


def _attn_sample(page_table, q_cat, ckv_new, kpe_new, cache_lat, cache_rope_t):
    s_, rows, _ = q_cat.shape
    t_new = ckv_new.shape[1]
    n_pages = page_table.shape[1]
    pages = math.gcd(n_pages, 32)
    group = math.gcd(pages, 16)
    n_chunks = n_pages // pages
    assert MLA_HEADS == 8 and rows == t_new * MLA_HEADS and t_new <= LANES
    assert n_chunks % 2 == 0 or s_ == 1, "buffer slot parity must carry across sequences"
    keys = pages * PAGE_SIZE
    est = (2 * pages * (_nbytes((PAGE_SIZE, KV_LORA), F32) + _nbytes((QK_ROPE, PAGE_SIZE), F32))
           + _nbytes((keys, KV_LORA), BF16) + _nbytes((QK_ROPE, keys), BF16)
           + 8 * _nbytes((rows, group * PAGE_SIZE), F32) + 8 * _nbytes((rows, KV_LORA), F32)
           + 4 * _nbytes((rows, QK_CAT), BF16))
    grid_spec = pltpu.PrefetchScalarGridSpec(
        num_scalar_prefetch=1,
        grid=(s_,),
        in_specs=[
            pl.BlockSpec((None, rows, QK_CAT), lambda b, pt: (b, 0, 0)),
            pl.BlockSpec((None, t_new, KV_LORA), lambda b, pt: (b, 0, 0)),
            pl.BlockSpec((None, t_new, QK_ROPE), lambda b, pt: (b, 0, 0)),
            pl.BlockSpec(memory_space=pl.ANY),
            pl.BlockSpec(memory_space=pl.ANY),
        ],
        out_specs=pl.BlockSpec((None, rows, KV_LORA), lambda b, pt: (b, 0, 0)),
        scratch_shapes=[
            pltpu.VMEM((2, pages, PAGE_SIZE, KV_LORA), F32),
            pltpu.VMEM((2, pages, QK_ROPE, PAGE_SIZE), F32),
            pltpu.SemaphoreType.DMA((2, 2)),
            pltpu.VMEM((keys, KV_LORA), BF16), pltpu.VMEM((QK_ROPE, keys), BF16),
            pltpu.VMEM((rows, LANES), F32), pltpu.VMEM((rows, LANES), F32),
            pltpu.VMEM((rows, KV_LORA), F32)],
    )
    return pl.pallas_call(
        functools.partial(_attn_sample_kernel, pages=pages, group=group, n_chunks=n_chunks),
        grid_spec=grid_spec,
        out_shape=jax.ShapeDtypeStruct((s_, rows, KV_LORA), BF16),
        compiler_params=pltpu.CompilerParams(
            dimension_semantics=("arbitrary",), vmem_limit_bytes=_vmem_limit(est)),
        name="attn_sample",
    )(page_table.reshape(-1), q_cat, ckv_new, kpe_new, cache_lat, cache_rope_t)


def _hgrn_chunk(hq, hf, v, hg, st, lb, g_out, tri):
    c = hq.shape[0]
    sub = min(HG_SUB, c)
    q = hq * _sigmoid(hq)
    log_f = jnp.log(lb + (1.0 - lb) * _sigmoid(hf))
    k = (1.0 - lb) * _sigmoid(-hf)
    b = jnp.dot(tri, log_f, preferred_element_type=F32, precision=lax.Precision.HIGHEST)
    vb = v.astype(BF16)
    stb = st.astype(BF16)
    o_parts = []
    row = lax.broadcasted_iota(jnp.int32, (SUBLANES, hq.shape[1]), 0)
    n_rg = sub // SUBLANES
    for i in range(c // sub):
        lo = i * sub
        acc = [jnp.zeros((SUBLANES, v.shape[1]), F32) for _ in range(n_rg)]
        for s in range(sub):
            ks, bs, vs = k[lo + s:lo + s + 1], b[lo + s:lo + s + 1], v[lo + s:lo + s + 1]
            for rg in range(s // SUBLANES, n_rg):
                r0 = lo + rg * SUBLANES
                e = jnp.exp(b[r0:r0 + SUBLANES] - bs)
                if rg == s // SUBLANES:
                    e = jnp.where(row >= s % SUBLANES, e, 0.0)
                a_col = jnp.sum(q[r0:r0 + SUBLANES] * ks * e, axis=1, keepdims=True)
                acc[rg] = acc[rg] + a_col * vs
        o_i = acc[0] if n_rg == 1 else jnp.concatenate(acc, axis=0)
        if i > 0:
            r = b[lo - 1:lo]
            qh = (q[lo:lo + sub] * jnp.exp(b[lo:lo + sub] - r)).astype(BF16)
            kh = (k[:lo] * jnp.exp(r - b[:lo])).astype(BF16)
            a_off = lax.dot_general(qh, kh, _NT, preferred_element_type=F32)
            o_i = o_i + _dot(a_off.astype(BF16), vb[:lo])
        o_parts.append(o_i)
    o = o_parts[0] if len(o_parts) == 1 else jnp.concatenate(o_parts, axis=0)
    o = o + lax.dot_general((q * jnp.exp(b)).astype(BF16), stb, _NT, preferred_element_type=F32)
    b_last = b[c - 1:c]
    kd = (k * jnp.exp(b_last - b)).astype(BF16)
    st_new = st * jnp.exp(b_last) + lax.dot_general(vb, kd, _TN, preferred_element_type=F32)
    o = _rms(o, g_out) * (hg * _sigmoid(hg))
    return o, st_new


def _hgrn_lb(lb_ref):
    a = lb_ref[...]
    a0, a1 = a[0:1], a[1:2]
    m = jnp.maximum(a0, a1)
    e0, e1 = jnp.exp(a0 - m), jnp.exp(a1 - m)
    return e0 / (e0 + e1)


def _tri(c):
    return (lax.broadcasted_iota(jnp.int32, (c, c), 0)
            >= lax.broadcasted_iota(jnp.int32, (c, c), 1)).astype(F32)


def _hgrn_prompt_kernel(hq_ref, hf_ref, hi_ref, hg_ref, lb_ref, g_ref, o_ref, s_ref, st_scr,
                        *, chunk, n_chunks, heads):
    step = pl.program_id(2)

    @pl.when(step == 0)
    def _():
        st_scr[...] = jnp.zeros(st_scr.shape, F32)

    lb = _hgrn_lb(lb_ref)
    g = g_ref[...]
    tri = _tri(chunk)

    def body(ci, carry):
        rows = pl.ds(pl.multiple_of(ci * chunk, chunk), chunk)
        for h in range(heads):
            cols = slice(h * HG_DK, (h + 1) * HG_DK)
            o, st_new = _hgrn_chunk(hq_ref[rows, cols], hf_ref[rows, cols], hi_ref[rows, cols],
                                    hg_ref[rows, cols], st_scr[h], lb[:, cols], g, tri)
            o_ref[rows, cols] = o.astype(o_ref.dtype)
            st_scr[h] = st_new
        return carry

    lax.fori_loop(0, n_chunks, body, 0)

    @pl.when(step == pl.num_programs(2) - 1)
    def _():
        for h in range(heads):
            s_ref[h] = st_scr[h].T


def _hgrn_prompt(z, hg_lb, g_out, b, t):
    chunk = math.gcd(t, HG_CHUNK)
    n_chunks = min(4, t // chunk)
    tb = chunk * n_chunks
    heads = 8
    assert t % tb == 0 and HG_HEADS % heads == 0
    steps = t // tb
    w = heads * HG_DK
    col0 = (Q_LORA + KV_LORA) // w
    groups = HG_HEADS // heads

    def zspec(group):
        return pl.BlockSpec((tb, w), lambda bi, h, s: (bi * steps + s, col0 + group * groups + h))

    est = (2 * (4 * _nbytes((tb, w), F32) + _nbytes((tb, w), BF16))
           + (3 + 64) * heads * _nbytes((HG_DK, HG_DV), F32))
    return pl.pallas_call(
        functools.partial(_hgrn_prompt_kernel, chunk=chunk, n_chunks=n_chunks, heads=heads),
        grid=(b, groups, steps),
        in_specs=[zspec(0), zspec(1), zspec(2), zspec(3),
                  pl.BlockSpec((2, w), lambda bi, h, s: (0, h)),
                  pl.BlockSpec((1, HG_DV), lambda bi, h, s: (0, 0))],
        out_specs=[pl.BlockSpec((tb, w), lambda bi, h, s: (bi * steps + s, h)),
                   pl.BlockSpec((None, heads, HG_DK, HG_DV), lambda bi, h, s: (bi, h, 0, 0))],
        out_shape=[jax.ShapeDtypeStruct((b * t, HG_HEADS * HG_DV), BF16),
                   jax.ShapeDtypeStruct((b, HG_HEADS, HG_DK, HG_DV), F32)],
        scratch_shapes=[pltpu.VMEM((heads, HG_DV, HG_DK), F32)],
        compiler_params=pltpu.CompilerParams(
            dimension_semantics=("parallel", "parallel", "arbitrary"),
            vmem_limit_bytes=_vmem_limit(est)),
        name="hgrn_prompt",
    )(z, z, z, z, hg_lb, g_out)


def _hgrn_sample_kernel(hq_ref, hf_ref, hi_ref, hg_ref, s0_ref, lb_ref, g_ref, o_ref, s_ref):
    lb_all = _hgrn_lb(lb_ref)
    g = g_ref[...]
    tri = _tri(hq_ref.shape[0])
    for h in range(HG_HEADS):
        cols = slice(h * HG_DK, (h + 1) * HG_DK)
        o, st_new = _hgrn_chunk(hq_ref[:, cols], hf_ref[:, cols], hi_ref[:, cols], hg_ref[:, cols],
                                s0_ref[h].T, lb_all[:, cols], g, tri)
        o_ref[:, cols] = o
        s_ref[h] = st_new.T


def _hgrn_sample(z, state, hg_lb, g_out, t):
    nseq = state.shape[0]
    assert HG_CHUNK % t == 0 and t % SUBLANES == 0
    w = HG_HEADS * HG_DK
    col0 = (Q_LORA + KV_LORA) // w
    zspec = lambda group: pl.BlockSpec((t, w), lambda bi: (bi, col0 + group))
    sspec = pl.BlockSpec((None, HG_HEADS, HG_DK, HG_DV), lambda bi: (bi, 0, 0, 0))
    est = 2 * (5 * _nbytes((t, w), F32) + 2 * _nbytes((HG_HEADS, HG_DK, HG_DV), F32)) + 64 * _nbytes((HG_DK, HG_DV), F32)
    return pl.pallas_call(
        _hgrn_sample_kernel,
        grid=(nseq,),
        in_specs=[zspec(0), zspec(1), zspec(2), zspec(3), sspec,
                  pl.BlockSpec((2, w), lambda bi: (0, 0)),
                  pl.BlockSpec((1, HG_DV), lambda bi: (0, 0))],
        out_specs=[pl.BlockSpec((t, w), lambda bi: (bi, 0)), sspec],
        out_shape=[jax.ShapeDtypeStruct((nseq * t, w), F32),
                   jax.ShapeDtypeStruct(state.shape, F32)],
        compiler_params=pltpu.CompilerParams(
            dimension_semantics=("parallel",), vmem_limit_bytes=_vmem_limit(est)),
        name="hgrn_sample",
    )(z, z, z, z, state, hg_lb, g_out)


def _out_proj_kernel(x_ref, olat_ref, ohg_ref, wuv_ref, gm_ref, wo_ref, h_ref):
    parts = []
    for h in range(MLA_HEADS):
        om = _dot(olat_ref[:, h * KV_LORA:(h + 1) * KV_LORA], wuv_ref[h])
        parts.append(_rms(om, gm_ref[...]).astype(BF16))
    o_mla = jnp.concatenate(parts, axis=1)
    w = MLA_HEADS * V_HEAD
    h_ref[...] = (x_ref[...] + _dot(o_mla, wo_ref[:w, :])
                  + _dot(ohg_ref[...].astype(BF16), wo_ref[w:, :]))


def _out_proj(x, o_lat, o_hg, w_uvT, g_mla, w_o):
    n, d = x.shape
    tm = min(256, n)
    assert n % tm == 0
    lw = MLA_HEADS * KV_LORA
    hw = o_hg.shape[1]
    est = (2 * (2 * _nbytes((tm, d), F32) + _nbytes((tm, lw), BF16) + _nbytes((tm, hw), o_hg.dtype)
                + _nbytes(w_uvT.shape, BF16) + _nbytes(w_o.shape, BF16))
           + 4 * _nbytes((tm, d), F32))
    full = lambda shape: pl.BlockSpec(shape, lambda i: (0,) * len(shape))
    return pl.pallas_call(
        _out_proj_kernel,
        grid=(n // tm,),
        in_specs=[pl.BlockSpec((tm, d), lambda i: (i, 0)),
                  pl.BlockSpec((tm, lw), lambda i: (i, 0)),
                  pl.BlockSpec((tm, hw), lambda i: (i, 0)),
                  full(w_uvT.shape), full((1, V_HEAD)), full(w_o.shape)],
        out_specs=pl.BlockSpec((tm, d), lambda i: (i, 0)),
        out_shape=jax.ShapeDtypeStruct((n, d), F32),
        compiler_params=pltpu.CompilerParams(
            dimension_semantics=("parallel",), vmem_limit_bytes=_vmem_limit(est)),
        name="out_proj",
    )(x, o_lat, o_hg, w_uvT, g_mla, w_o)


def _ffn_kernel(h_ref, g_ref, wg_ref, wu_ref, wd_ref, o_ref, hn_scr):
    @pl.when(pl.program_id(1) == 0)
    def _():
        h = h_ref[...]
        hn_scr[...] = _rms(h, g_ref[...]).astype(BF16)
        o_ref[...] = h

    hn = hn_scr[...]
    gate = _dot(hn, wg_ref[...])
    up = _dot(hn, wu_ref[...])
    act = (gate * _sigmoid(gate) * up).astype(BF16)
    o_ref[...] += _dot(act, wd_ref[...])


def _ffn(h, g, w_gate, w_up, w_down):
    n, d = h.shape
    dff = w_gate.shape[1]
    tm = min(512, n)
    tf = 512
    assert n % tm == 0 and dff % tf == 0
    est = (2 * (2 * _nbytes((tm, d), F32) + 3 * _nbytes((d, tf), BF16))
           + _nbytes((tm, d), BF16) + 4 * _nbytes((tm, tf), F32) + _nbytes((tm, d), F32))
    return pl.pallas_call(
        _ffn_kernel,
        grid=(n // tm, dff // tf),
        in_specs=[pl.BlockSpec((tm, d), lambda i, j: (i, 0)),
                  pl.BlockSpec((1, d), lambda i, j: (0, 0)),
                  pl.BlockSpec((d, tf), lambda i, j: (0, j)),
                  pl.BlockSpec((d, tf), lambda i, j: (0, j)),
                  pl.BlockSpec((tf, d), lambda i, j: (j, 0))],
        out_specs=pl.BlockSpec((tm, d), lambda i, j: (i, 0)),
        out_shape=jax.ShapeDtypeStruct((n, d), F32),
        scratch_shapes=[pltpu.VMEM((tm, d), BF16)],
        compiler_params=pltpu.CompilerParams(
            dimension_semantics=("parallel", "arbitrary"), vmem_limit_bytes=_vmem_limit(est)),
        name="ffn",
    )(h, g, w_gate, w_up, w_down)


def _ple_final_kernel(h_ref, p_ref, gp_ref, wg_ref, wp_ref, gf_ref, y_ref):
    h = h_ref[...]
    a = _dot(_rms(h, gp_ref[...]).astype(BF16), wg_ref[...])
    pp = _dot(p_ref[...].astype(BF16), wp_ref[...])
    y_ref[...] = _rms(h + pp * _sigmoid(a), gf_ref[...])


def _ple_final(h, p, g_ple, w_gate, w_proj, g_final):
    n, d = h.shape
    pd = p.shape[1]
    tm = min(256, n)
    assert n % tm == 0
    est = (2 * (2 * _nbytes((tm, d), F32) + _nbytes((tm, pd), F32)
                + _nbytes(w_gate.shape, BF16) + _nbytes(w_proj.shape, BF16))
           + 4 * _nbytes((tm, d), F32))
    full = lambda shape: pl.BlockSpec(shape, lambda i: (0,) * len(shape))
    return pl.pallas_call(
        _ple_final_kernel,
        grid=(n // tm,),
        in_specs=[pl.BlockSpec((tm, d), lambda i: (i, 0)),
                  pl.BlockSpec((tm, pd), lambda i: (i, 0)),
                  full((1, d)), full(w_gate.shape), full(w_proj.shape), full((1, d))],
        out_specs=pl.BlockSpec((tm, d), lambda i: (i, 0)),
        out_shape=jax.ShapeDtypeStruct((n, d), F32),
        compiler_params=pltpu.CompilerParams(
            dimension_semantics=("parallel",), vmem_limit_bytes=_vmem_limit(est)),
        name="ple_final",
    )(h, p, g_ple, w_gate, w_proj, g_final)


def _rope_tables(pos, reps):
    inv = ROPE_BASE ** (-jnp.arange(0, QK_ROPE, 2, dtype=F32) / QK_ROPE)
    ang = pos.astype(F32)[:, None] * inv[None, :]
    cos, sin = jnp.cos(ang), jnp.sin(ang)
    cos128 = jnp.tile(jnp.concatenate([cos, cos], axis=1), (reps, LANES // QK_ROPE))
    sin128 = jnp.tile(jnp.concatenate([-sin, sin], axis=1), (reps, LANES // QK_ROPE))
    return cos128, sin128


def _prep_weights(g_mix, w_in, g_q, w_uq, g_kv, w_uk, w_uv, g_mla_out, g_hg_out, w_o, g_ffn,
                  w_ffn_gate, w_ffn_up, w_ffn_down, g_ple, w_ple_gate, w_ple_proj, g_final):
    row = lambda g: g.reshape(1, -1).astype(F32)
    half = QK_ROPE // 2
    lo = Q_LORA + KV_LORA
    w_main = jnp.concatenate([w_in[:, :lo], w_in[:, lo + QK_ROPE:]], axis=1).astype(BF16)
    w_k = w_in[:, lo:lo + QK_ROPE]
    lane_pad = lambda a: jnp.pad(a, [(0, 0)] * (a.ndim - 1) + [(0, LANES - a.shape[-1])])
    w_kpe = jnp.concatenate([lane_pad(w_k), lane_pad(jnp.roll(w_k, half, axis=1))], axis=1).astype(BF16)
    w_nope = w_uq[:, :, :QK_NOPE].reshape(Q_LORA, MLA_HEADS * QK_NOPE).astype(BF16)
    pe = w_uq[:, :, QK_NOPE:]
    w_pe = lane_pad(pe).reshape(Q_LORA, MLA_HEADS * LANES).astype(BF16)
    w_pe_rot = lane_pad(jnp.roll(pe, half, axis=2)).reshape(Q_LORA, MLA_HEADS * LANES).astype(BF16)
    w_ukT = jnp.transpose(w_uk, (1, 2, 0)).astype(BF16)
    w_uvT = jnp.transpose(w_uv, (1, 0, 2)).astype(BF16)
    return dict(
        g_mix=row(g_mix), w_main=w_main, w_kpe=w_kpe, g_q=row(g_q), g_kv=row(g_kv),
        w_nope=w_nope, w_pe=w_pe, w_pe_rot=w_pe_rot, w_ukT=w_ukT, w_uvT=w_uvT,
        g_mla=row(g_mla_out), g_hg=row(g_hg_out), w_o=w_o.astype(BF16), g_ffn=row(g_ffn),
        w_gate=w_ffn_gate.astype(BF16), w_up=w_ffn_up.astype(BF16), w_down=w_ffn_down.astype(BF16),
        g_ple=row(g_ple), w_ple_gate=w_ple_gate.astype(BF16), w_ple_proj=w_ple_proj.astype(BF16),
        g_final=row(g_final))


def _front(x, pos, w):
    b, t, d = x.shape
    cos128, sin128 = _rope_tables(pos, b)
    x2 = x.reshape(b * t, d)
    z, kpe, kpe_b = _in_proj(x2, w["g_mix"], cos128, sin128, w["w_main"], w["w_kpe"])
    q_cat, k_cat, ckv = _mla_q(z, kpe_b, cos128, sin128, w["g_q"], w["g_kv"],
                               w["w_nope"], w["w_pe"], w["w_pe_rot"], w["w_ukT"])
    return x2, z, q_cat, k_cat, ckv, kpe


def _back(x2, o_lat, o_hg, p, w):
    n = x2.shape[0]
    h = _out_proj(x2, o_lat, o_hg, w["w_uvT"], w["g_mla"], w["w_o"])
    h = _ffn(h, w["g_ffn"], w["w_gate"], w["w_up"], w["w_down"])
    return _ple_final(h, p.reshape(n, -1), w["g_ple"], w["w_ple_gate"], w["w_ple_proj"], w["g_final"])


def kernel(x_prompt, x_sample, cache_kv_latent, cache_k_rope, state_hgrn, page_table, p_prompt, p_sample, g_mix, w_in, g_q, w_uq, g_kv, w_uk, w_uv, g_mla_out, hg_lb, g_hg_out, w_o, g_ffn, w_ffn_gate, w_ffn_up, w_ffn_down, g_ple, w_ple_gate, w_ple_proj, g_final):
    depth = g_mix.shape[0]
    assert depth == 1 and hg_lb.shape[0] == 2, "single-layer trunk only"
    w = _prep_weights(g_mix[0], w_in[0], g_q[0], w_uq[0], g_kv[0], w_uk[0], w_uv[0], g_mla_out[0],
                      g_hg_out[0], w_o[0], g_ffn[0], w_ffn_gate[0], w_ffn_up[0], w_ffn_down[0],
                      g_ple[0], w_ple_gate[0], w_ple_proj[0], g_final)
    hg_lb = hg_lb.astype(F32)
    bp, tp, d = x_prompt.shape
    bs, ts, _ = x_sample.shape
    past_len = page_table.shape[1] * PAGE_SIZE

    x2, z, q_cat, k_cat, ckv, kpe = _front(x_prompt, jnp.arange(tp, dtype=jnp.int32), w)
    o_lat = _attn_prompt(q_cat, k_cat, bp, tp)
    o_hg, st_p = _hgrn_prompt(z, hg_lb, w["g_hg"], bp, tp)
    y_prompt = _back(x2, o_lat, o_hg, p_prompt, w).reshape(bp, tp, d)
    lat_p = ckv.reshape(1, bp, tp, KV_LORA)
    rope_p = kpe.reshape(1, bp, tp, QK_ROPE)

    pos_s = past_len + jnp.arange(ts, dtype=jnp.int32)
    x2, z, q_cat, k_cat, ckv, kpe = _front(x_sample, pos_s, w)
    n_pool = cache_kv_latent.shape[1]
    o_lat = _attn_sample(page_table, q_cat.reshape(bs, ts * MLA_HEADS, QK_CAT),
                         ckv.reshape(bs, ts, KV_LORA), kpe.reshape(bs, ts, QK_ROPE),
                         cache_kv_latent.reshape(n_pool, PAGE_SIZE, KV_LORA),
                         jnp.swapaxes(cache_k_rope.reshape(n_pool, PAGE_SIZE, QK_ROPE), 1, 2))
    o_hg, st_s = _hgrn_sample(z, state_hgrn.reshape(state_hgrn.shape[1:]), hg_lb, w["g_hg"], ts)
    y_sample = _back(x2, o_lat.reshape(bs * ts, MLA_HEADS * KV_LORA), o_hg, p_sample, w).reshape(bs, ts, d)
    lat_s = ckv.reshape(1, bs, ts, KV_LORA)
    rope_s = kpe.reshape(1, bs, ts, QK_ROPE)

    return (y_prompt, y_sample, lat_p, rope_p, st_p[None], lat_s, rope_s, st_s[None])
```

```python
import functools
import math

import jax
import jax.numpy as jnp
from jax import lax
from jax.experimental import pallas as pl
from jax.experimental.pallas import tpu as pltpu

F32 = jnp.float32
BF16 = jnp.bfloat16

EPS = 1e-6
ROPE_BASE = 10000.0
PAGE_SIZE = 128
MLA_HEADS = 8
V_HEAD = 128
QK_NOPE = 128
QK_ROPE = 64
Q_LORA = 512
KV_LORA = 512
ATTN_SCALE = (QK_NOPE + QK_ROPE) ** -0.5
HG_HEADS = 8
HG_DK = 128
HG_DV = 128
HG_CHUNK = 64
HG_SUB = 16
LOG2E = math.log2(math.e)

V7X_VMEM_BYTES = 64 * 1024 * 1024
LANES = 128
SUBLANES = 8

QK_CAT = KV_LORA + LANES

_NT = (((1,), (1,)), ((), ()))
_TN = (((0,), (0,)), ((), ()))


def _vmem_limit(nbytes):
    return int(min(max(nbytes, 16 * 1024 * 1024), V7X_VMEM_BYTES - 8 * 1024 * 1024))


def _nbytes(shape, dtype):
    return math.prod(shape) * jnp.dtype(dtype).itemsize


def _rms(x, g):
    ms = jnp.mean(x * x, axis=-1, keepdims=True)
    return x * lax.rsqrt(ms + EPS) * g


def _dot(a, b):
    return jnp.dot(a, b, preferred_element_type=F32)


def _sigmoid(x):
    return 1.0 / (1.0 + jnp.exp(-x))


def _in_proj_kernel(x_ref, g_ref, cos_ref, sin_ref, wm_ref, wk_ref,
                    z_ref, kpe_ref, kpeb_ref, u_scr):
    @pl.when(pl.program_id(1) == 0)
    def _():
        u = _rms(x_ref[...], g_ref[...]).astype(BF16)
        u_scr[...] = u
        kk = _dot(u, wk_ref[...])
        kpe = kk[:, :LANES] * cos_ref[...] + kk[:, LANES:] * sin_ref[...]
        kpe_ref[...] = kpe[:, :QK_ROPE]
        kpeb_ref[...] = kpe.astype(BF16)

    z_ref[...] = _dot(u_scr[...], wm_ref[...])


def _in_proj(x, g, cos128, sin128, w_main, w_kpe):
    n, d = x.shape
    nout = w_main.shape[1]
    tm = min(1024, n)
    tn = 1024
    assert n % tm == 0 and nout % tn == 0
    est = (2 * (_nbytes((tm, d), F32) + _nbytes((d, tn), BF16) + _nbytes((tm, tn), F32))
           + 2 * _nbytes((tm, d), BF16) + _nbytes((tm, d), F32))
    return pl.pallas_call(
        _in_proj_kernel,
        grid=(n // tm, nout // tn),
        in_specs=[
            pl.BlockSpec((tm, d), lambda i, j: (i, 0)),
            pl.BlockSpec((1, d), lambda i, j: (0, 0)),
            pl.BlockSpec((tm, LANES), lambda i, j: (i, 0)),
            pl.BlockSpec((tm, LANES), lambda i, j: (i, 0)),
            pl.BlockSpec((d, tn), lambda i, j: (0, j)),
            pl.BlockSpec((d, 2 * LANES), lambda i, j: (0, 0)),
        ],
        out_specs=[
            pl.BlockSpec((tm, tn), lambda i, j: (i, j)),
            pl.BlockSpec((tm, QK_ROPE), lambda i, j: (i, 0)),
            pl.BlockSpec((tm, LANES), lambda i, j: (i, 0)),
        ],
        out_shape=[
            jax.ShapeDtypeStruct((n, nout), F32),
            jax.ShapeDtypeStruct((n, QK_ROPE), F32),
            jax.ShapeDtypeStruct((n, LANES), BF16),
        ],
        scratch_shapes=[pltpu.VMEM((tm, d), BF16)],
        compiler_params=pltpu.CompilerParams(
            dimension_semantics=("parallel", "arbitrary"),
            vmem_limit_bytes=_vmem_limit(est)),
        name="in_proj",
    )(x, g, cos128, sin128, w_main, w_kpe)


def _mla_q_kernel(cq_ref, ckv_ref, kpeb_ref, cos_ref, sin_ref, gq_ref, gkv_ref,
                  wn_ref, wp_ref, wpr_ref, wuk_ref,
                  qcat_ref, kcat_ref, ckv_out_ref):
    cqn = _rms(cq_ref[...], gq_ref[...]).astype(BF16)
    qn = _dot(cqn, wn_ref[...]).astype(BF16)
    x = _dot(cqn, wp_ref[...])
    xr = _dot(cqn, wpr_ref[...])
    c, s = cos_ref[...], sin_ref[...]
    for h in range(MLA_HEADS):
        col = h * QK_CAT
        qa = _dot(qn[:, h * QK_NOPE:(h + 1) * QK_NOPE], wuk_ref[h])
        qcat_ref[:, col:col + KV_LORA] = (qa * ATTN_SCALE).astype(BF16)
        lanes = slice(h * LANES, (h + 1) * LANES)
        qcat_ref[:, col + KV_LORA:col + QK_CAT] = (
            (x[:, lanes] * c + xr[:, lanes] * s) * ATTN_SCALE).astype(BF16)
    ckv = _rms(ckv_ref[...], gkv_ref[...])
    ckv_out_ref[...] = ckv
    kcat_ref[:, :KV_LORA] = ckv.astype(BF16)
    kcat_ref[:, KV_LORA:] = kpeb_ref[...]


def _mla_q(z, kpe_b, cos128, sin128, g_q, g_kv, w_nope, w_pe, w_pe_rot, w_ukT):
    n = z.shape[0]
    tm = min(512, n)
    assert n % tm == 0
    qw = MLA_HEADS * QK_CAT
    est = (2 * (2 * _nbytes((tm, Q_LORA), F32) + 2 * _nbytes((tm, LANES), F32) + _nbytes((tm, LANES), BF16)
                + _nbytes((tm, qw), BF16) + _nbytes((tm, QK_CAT), BF16) + _nbytes((tm, KV_LORA), F32)
                + 3 * _nbytes(w_nope.shape, BF16) + _nbytes(w_ukT.shape, BF16))
           + 6 * _nbytes((tm, MLA_HEADS * QK_NOPE), F32))
    full = lambda shape: pl.BlockSpec(shape, lambda i: (0,) * len(shape))
    return pl.pallas_call(
        _mla_q_kernel,
        grid=(n // tm,),
        in_specs=[
            pl.BlockSpec((tm, Q_LORA), lambda i: (i, 0)),
            pl.BlockSpec((tm, KV_LORA), lambda i: (i, 1)),
            pl.BlockSpec((tm, LANES), lambda i: (i, 0)),
            pl.BlockSpec((tm, LANES), lambda i: (i, 0)),
            pl.BlockSpec((tm, LANES), lambda i: (i, 0)),
            full((1, Q_LORA)), full((1, KV_LORA)),
            full(w_nope.shape), full(w_pe.shape), full(w_pe_rot.shape), full(w_ukT.shape),
        ],
        out_specs=[
            pl.BlockSpec((tm, qw), lambda i: (i, 0)),
            pl.BlockSpec((tm, QK_CAT), lambda i: (i, 0)),
            pl.BlockSpec((tm, KV_LORA), lambda i: (i, 0)),
        ],
        out_shape=[
            jax.ShapeDtypeStruct((n, qw), BF16),
            jax.ShapeDtypeStruct((n, QK_CAT), BF16),
            jax.ShapeDtypeStruct((n, KV_LORA), F32),
        ],
        compiler_params=pltpu.CompilerParams(
            dimension_semantics=("parallel",), vmem_limit_bytes=_vmem_limit(est)),
        name="mla_q",
    )(z, z, kpe_b, cos128, sin128, g_q, g_kv, w_nope, w_pe, w_pe_rot, w_ukT)


def _lane_tile(x, width):
    reps = width // LANES
    return x if reps == 1 else jnp.concatenate([x] * reps, axis=1)


def _softmax_update(s, v, m_scr, l_scr, acc_scr):
    m_prev = m_scr[...]
    m_new = jnp.maximum(m_prev, jnp.max(s, axis=1, keepdims=True))
    alpha = jnp.exp(m_prev - m_new)
    p = jnp.exp(s - _lane_tile(m_new, s.shape[1]))
    l_scr[...] = alpha * l_scr[...] + jnp.sum(p, axis=1, keepdims=True)
    m_scr[...] = m_new
    acc_scr[...] = (acc_scr[...] * _lane_tile(alpha, acc_scr.shape[1])
                    + _dot(p.astype(BF16), v))


def _softmax_init(m_scr, l_scr, acc_scr):
    m_scr[...] = jnp.full(m_scr.shape, -jnp.inf, F32)
    l_scr[...] = jnp.zeros(l_scr.shape, F32)
    acc_scr[...] = jnp.zeros(acc_scr.shape, F32)


def _softmax_result(l_scr, acc_scr):
    return acc_scr[...] / _lane_tile(l_scr[...], acc_scr.shape[1])


def _attn_prompt_kernel(q_ref, k_ref, o_ref, qs_scr, m_scr, l_scr, acc_scr, *, tq, tk):
    i = pl.program_id(1)
    rows = tq * MLA_HEADS
    for h in range(MLA_HEADS):
        qs_scr[h * tq:(h + 1) * tq, :] = q_ref[:, h * QK_CAT:(h + 1) * QK_CAT]
    _softmax_init(m_scr, l_scr, acc_scr)

    def scores(start):
        return lax.dot_general(qs_scr[...], k_ref[pl.ds(start, tk), :], _NT,
                               preferred_element_type=F32)

    def full_chunk(c, carry):
        start = pl.multiple_of(c * tk, tk)
        _softmax_update(scores(start), k_ref[pl.ds(start, tk), :KV_LORA], m_scr, l_scr, acc_scr)
        return carry

    n_full = (i * tq) // tk
    lax.fori_loop(0, n_full, full_chunk, 0)

    start = pl.multiple_of(n_full * tk, tk)
    row_tok = i * tq + jnp.bitwise_and(lax.broadcasted_iota(jnp.int32, (rows, tk), 0), tq - 1)
    col = start + lax.broadcasted_iota(jnp.int32, (rows, tk), 1)
    s = jnp.where(col <= row_tok, scores(start), -jnp.inf)
    _softmax_update(s, k_ref[pl.ds(start, tk), :KV_LORA], m_scr, l_scr, acc_scr)
    res = _softmax_result(l_scr, acc_scr).astype(o_ref.dtype)
    for h in range(MLA_HEADS):
        o_ref[:, h * KV_LORA:(h + 1) * KV_LORA] = res[h * tq:(h + 1) * tq, :]


def _attn_prompt(q_cat, k_cat, b, t):
    tq = min(128, t)
    tk = min(512, t)
    assert t % tk == 0 and tk % tq == 0 and tq & (tq - 1) == 0
    rows = tq * MLA_HEADS
    qw, ow = MLA_HEADS * QK_CAT, MLA_HEADS * KV_LORA
    est = (2 * (_nbytes((tq, qw), BF16) + _nbytes((t, QK_CAT), BF16) + _nbytes((tq, ow), BF16))
           + _nbytes((rows, QK_CAT), BF16) + 2 * _nbytes((rows, LANES), F32)
           + _nbytes((rows, KV_LORA), F32) + 6 * _nbytes((rows, tk), F32))
    nq = t // tq
    return pl.pallas_call(
        functools.partial(_attn_prompt_kernel, tq=tq, tk=tk),
        grid=(b, nq),
        in_specs=[
            pl.BlockSpec((tq, qw), lambda bi, i: (bi * nq + i, 0)),
            pl.BlockSpec((t, QK_CAT), lambda bi, i: (bi, 0)),
        ],
        out_specs=pl.BlockSpec((tq, ow), lambda bi, i: (bi * nq + i, 0)),
        out_shape=jax.ShapeDtypeStruct((b * t, ow), BF16),
        scratch_shapes=[pltpu.VMEM((rows, QK_CAT), BF16),
                        pltpu.VMEM((rows, LANES), F32), pltpu.VMEM((rows, LANES), F32),
                        pltpu.VMEM((rows, KV_LORA), F32)],
        compiler_params=pltpu.CompilerParams(
            dimension_semantics=("parallel", "parallel"), vmem_limit_bytes=_vmem_limit(est)),
        name="attn_prompt",
    )(q_cat, k_cat)


def _attn_sample_kernel(pt_ref, q_ref, cn_ref, kn_ref, lat_hbm, rope_hbm, o_ref,
                        lat_buf, rope_buf, sems, latb_scr, ropeb_scr, m_scr, l_scr, acc_scr,
                        *, pages, group, n_chunks):
    b = pl.program_id(0)
    rows, t_new = q_ref.shape[0], cn_ref.shape[0]

    streams = ((lat_hbm, lat_buf), (rope_hbm, rope_buf))

    def page_copy(stream, page, slot, k):
        hbm, buf = streams[stream]
        return pltpu.make_async_copy(hbm.at[page], buf.at[slot, k], sems.at[stream, slot])

    def start_chunk(g, slot):
        for k in range(pages):
            page = pt_ref[g * pages + k]
            for stream in range(2):
                page_copy(stream, page, slot, k).start()

    def wait_chunk(slot):
        for stream in range(2):
            for k in range(pages):
                page_copy(stream, 0, slot, k).wait()

    @pl.when(b == 0)
    def _():
        start_chunk(0, 0)

    qa = q_ref[:, :KV_LORA]
    qp = q_ref[:, KV_LORA:KV_LORA + QK_ROPE]
    _softmax_init(m_scr, l_scr, acc_scr)

    pending = None
    groups_per_chunk = pages // group
    for gi in range(n_chunks * groups_per_chunk):
        c, j = divmod(gi, groups_per_chunk)
        slot = c % 2
        if j == 0:
            start_chunk(b * n_chunks + c + 1, 1 - slot)
            wait_chunk(slot)
        if gi == 0:
            pad = LANES - t_new
            cn = jnp.concatenate([cn_ref[...], jnp.zeros((pad, KV_LORA), F32)], axis=0).astype(BF16)
            kn = jnp.concatenate([kn_ref[...], jnp.zeros((pad, QK_ROPE), F32)], axis=0).astype(BF16)
            sn = lax.dot_general(qa, cn, _NT, preferred_element_type=F32)
            sn = sn + lax.dot_general(qp, kn, _NT, preferred_element_type=F32)
            row_tok = lax.shift_right_logical(lax.broadcasted_iota(jnp.int32, (rows, LANES), 0), 3)
            col = lax.broadcasted_iota(jnp.int32, (rows, LANES), 1)
            pending = (jnp.where(col <= row_tok, sn, -jnp.inf), lambda cn=cn: cn)
        for k in range(j * group, (j + 1) * group):
            latb_scr[k * PAGE_SIZE:(k + 1) * PAGE_SIZE, :] = lat_buf[slot, k].astype(BF16)
            ropeb_scr[:, k * PAGE_SIZE:(k + 1) * PAGE_SIZE] = rope_buf[slot, k].astype(BF16)
        keys = slice(j * group * PAGE_SIZE, (j + 1) * group * PAGE_SIZE)
        s = lax.dot_general(qa, latb_scr[keys, :], _NT, preferred_element_type=F32)
        s = s + _dot(qp, ropeb_scr[:, keys])
        _softmax_update(pending[0], pending[1](), m_scr, l_scr, acc_scr)
        pending = (s, lambda keys=keys: latb_scr[keys, :])
    _softmax_update(pending[0], pending[1](), m_scr, l_scr, acc_scr)
    o_ref[...] = _softmax_result(l_scr, acc_scr).astype(o_ref.dtype)

    @pl.when(b == pl.num_programs(0) - 1)
    def _():
        wait_chunk(n_chunks % 2)


def _attn_sample(page_table, q_cat, ckv_new, kpe_new, cache_lat, cache_rope_t):
    s_, rows, _ = q_cat.shape
    t_new = ckv_new.shape[1]
    n_pages = page_table.shape[1]
    pages = math.gcd(n_pages, 32)
    group = math.gcd(pages, 8)
    n_chunks = n_pages // pages
    assert MLA_HEADS == 8 and rows == t_new * MLA_HEADS and t_new <= LANES
    assert n_chunks % 2 == 0, "buffer slot parity must carry across sequences"
    assert pages // group >= 2, "a group's bf16 keys must outlive the next group's conversion"
    pt_flat = page_table.reshape(-1)
    pt_flat = jnp.concatenate([pt_flat, pt_flat[:pages]])
    keys = pages * PAGE_SIZE
    est = (2 * pages * (_nbytes((PAGE_SIZE, KV_LORA), F32) + _nbytes((QK_ROPE, PAGE_SIZE), F32))
           + _nbytes((keys, KV_LORA), BF16) + _nbytes((QK_ROPE, keys), BF16)
           + 8 * _nbytes((rows, group * PAGE_SIZE), F32) + 8 * _nbytes((rows, KV_LORA), F32)
           + 4 * _nbytes((rows, QK_CAT), BF16))
    grid_spec = pltpu.PrefetchScalarGridSpec(
        num_scalar_prefetch=1,
        grid=(s_,),
        in_specs=[
            pl.BlockSpec((None, rows, QK_CAT), lambda b, pt: (b, 0, 0)),
            pl.BlockSpec((None, t_new, KV_LORA), lambda b, pt: (b, 0, 0)),
            pl.BlockSpec((None, t_new, QK_ROPE), lambda b, pt: (b, 0, 0)),
            pl.BlockSpec(memory_space=pl.ANY),
            pl.BlockSpec(memory_space=pl.ANY),
        ],
        out_specs=pl.BlockSpec((None, rows, KV_LORA), lambda b, pt: (b, 0, 0)),
        scratch_shapes=[
            pltpu.VMEM((2, pages, PAGE_SIZE, KV_LORA), F32),
            pltpu.VMEM((2, pages, QK_ROPE, PAGE_SIZE), F32),
            pltpu.SemaphoreType.DMA((2, 2)),
            pltpu.VMEM((keys, KV_LORA), BF16), pltpu.VMEM((QK_ROPE, keys), BF16),
            pltpu.VMEM((rows, LANES), F32), pltpu.VMEM((rows, LANES), F32),
            pltpu.VMEM((rows, KV_LORA), F32)],
    )
    return pl.pallas_call(
        functools.partial(_attn_sample_kernel, pages=pages, group=group, n_chunks=n_chunks),
        grid_spec=grid_spec,
        out_shape=jax.ShapeDtypeStruct((s_, rows, KV_LORA), BF16),
        compiler_params=pltpu.CompilerParams(
            dimension_semantics=("arbitrary",), vmem_limit_bytes=_vmem_limit(est)),
        name="attn_sample",
    )(pt_flat, q_cat, ckv_new, kpe_new, cache_lat, cache_rope_t)


def _hgrn_chunk(hq, hf, v, hg, sts, lb, g_out, tri):
    nh = len(sts)
    c = hq.shape[0]
    sub = min(HG_SUB, c)
    n_sub = c // sub
    n_rg = sub // SUBLANES
    hs = [slice(h * HG_DK, (h + 1) * HG_DK) for h in range(nh)]
    q = hq * _sigmoid(hq)
    log_f = jnp.log(lb + (1.0 - lb) * _sigmoid(hf))
    k = (1.0 - lb) * _sigmoid(-hf)
    b = jnp.dot(tri, log_f, preferred_element_type=F32, precision=lax.Precision.HIGHEST)
    b2 = b * LOG2E
    vb = v.astype(BF16)
    row = lax.broadcasted_iota(jnp.int32, (SUBLANES, hq.shape[1]), 0)
    o_sub = []
    for i in range(n_sub):
        lo = i * sub
        acc = [[jnp.zeros((SUBLANES, HG_DV), F32) for _ in range(n_rg)] for _ in range(nh)]
        for s in range(sub):
            ks, bs, vs = k[lo + s:lo + s + 1], b2[lo + s:lo + s + 1], v[lo + s:lo + s + 1]
            for rg in range(s // SUBLANES, n_rg):
                r0 = lo + rg * SUBLANES
                e = jnp.exp2(b2[r0:r0 + SUBLANES] - bs)
                if rg == s // SUBLANES:
                    e = jnp.where(row >= s % SUBLANES, e, 0.0)
                prod = q[r0:r0 + SUBLANES] * ks * e
                for h in range(nh):
                    a_col = jnp.sum(prod[:, hs[h]], axis=1, keepdims=True)
                    acc[h][rg] = acc[h][rg] + a_col * vs[:, hs[h]]
        o_i = [a[0] if n_rg == 1 else jnp.concatenate(a, axis=0) for a in acc]
        if i > 0:
            r = b[lo - 1:lo]
            qh = (q[lo:lo + sub] * jnp.exp(b[lo:lo + sub] - r)).astype(BF16)
            kh = (k[:lo] * jnp.exp(r - b[:lo])).astype(BF16)
            for h in range(nh):
                a_off = lax.dot_general(qh[:, hs[h]], kh[:, hs[h]], _NT,
                                        preferred_element_type=F32)
                o_i[h] = o_i[h] + _dot(a_off.astype(BF16), vb[:lo, hs[h]])
        o_sub.append(o_i)
    qe = (q * jnp.exp(b)).astype(BF16)
    b_last = b[c - 1:c]
    kd = (k * jnp.exp(b_last - b)).astype(BF16)
    decay = jnp.exp(b_last)
    gate = hg * _sigmoid(hg)
    outs, new_sts = [], []
    for h in range(nh):
        o = o_sub[0][h] if n_sub == 1 else jnp.concatenate([p[h] for p in o_sub], axis=0)
        o = o + lax.dot_general(qe[:, hs[h]], sts[h].astype(BF16), _NT, preferred_element_type=F32)
        new_sts.append(sts[h] * decay[:, hs[h]]
                       + lax.dot_general(vb[:, hs[h]], kd[:, hs[h]], _TN, preferred_element_type=F32))
        outs.append(_rms(o, g_out) * gate[:, hs[h]])
    return outs, new_sts


def _hgrn_lb(lb_ref):
    a = lb_ref[...]
    a0, a1 = a[0:1], a[1:2]
    m = jnp.maximum(a0, a1)
    e0, e1 = jnp.exp(a0 - m), jnp.exp(a1 - m)
    return e0 / (e0 + e1)


def _tri(c):
    return (lax.broadcasted_iota(jnp.int32, (c, c), 0)
            >= lax.broadcasted_iota(jnp.int32, (c, c), 1)).astype(F32)


def _hgrn_prompt_kernel(hq_ref, hf_ref, hi_ref, hg_ref, lb_ref, g_ref, o_ref, s_ref, st_scr,
                        *, chunk, n_chunks, heads):
    step = pl.program_id(2)

    @pl.when(step == 0)
    def _():
        st_scr[...] = jnp.zeros(st_scr.shape, F32)

    lb = _hgrn_lb(lb_ref)
    g = g_ref[...]
    tri = _tri(chunk)

    def body(ci, carry):
        rows = pl.ds(pl.multiple_of(ci * chunk, chunk), chunk)
        outs, new_sts = _hgrn_chunk(hq_ref[rows, :], hf_ref[rows, :], hi_ref[rows, :], hg_ref[rows, :],
                                    [st_scr[h] for h in range(heads)], lb, g, tri)
        o_ref[rows, :] = jnp.concatenate(outs, axis=1).astype(o_ref.dtype)
        for h in range(heads):
            st_scr[h] = new_sts[h]
        return carry

    lax.fori_loop(0, n_chunks, body, 0)

    @pl.when(step == pl.num_programs(2) - 1)
    def _():
        for h in range(heads):
            s_ref[h] = st_scr[h].T


def _hgrn_prompt(z, hg_lb, g_out, b, t):
    chunk = math.gcd(t, HG_CHUNK)
    n_chunks = min(4, t // chunk)
    tb = chunk * n_chunks
    heads = 8
    assert t % tb == 0 and HG_HEADS % heads == 0
    steps = t // tb
    w = heads * HG_DK
    col0 = (Q_LORA + KV_LORA) // w
    groups = HG_HEADS // heads

    def zspec(group):
        return pl.BlockSpec((tb, w), lambda bi, h, s: (bi * steps + s, col0 + group * groups + h))

    est = (2 * (4 * _nbytes((tb, w), F32) + _nbytes((tb, w), BF16))
           + (3 + 64) * heads * _nbytes((HG_DK, HG_DV), F32))
    return pl.pallas_call(
        functools.partial(_hgrn_prompt_kernel, chunk=chunk, n_chunks=n_chunks, heads=heads),
        grid=(b, groups, steps),
        in_specs=[zspec(0), zspec(1), zspec(2), zspec(3),
                  pl.BlockSpec((2, w), lambda bi, h, s: (0, h)),
                  pl.BlockSpec((1, HG_DV), lambda bi, h, s: (0, 0))],
        out_specs=[pl.BlockSpec((tb, w), lambda bi, h, s: (bi * steps + s, h)),
                   pl.BlockSpec((None, heads, HG_DK, HG_DV), lambda bi, h, s: (bi, h, 0, 0))],
        out_shape=[jax.ShapeDtypeStruct((b * t, HG_HEADS * HG_DV), BF16),
                   jax.ShapeDtypeStruct((b, HG_HEADS, HG_DK, HG_DV), F32)],
        scratch_shapes=[pltpu.VMEM((heads, HG_DV, HG_DK), F32)],
        compiler_params=pltpu.CompilerParams(
            dimension_semantics=("parallel", "parallel", "arbitrary"),
            vmem_limit_bytes=_vmem_limit(est)),
        name="hgrn_prompt",
    )(z, z, z, z, hg_lb, g_out)


def _hgrn_sample_kernel(hq_ref, hf_ref, hi_ref, hg_ref, s0_ref, lb_ref, g_ref, o_ref, s_ref):
    outs, new_sts = _hgrn_chunk(hq_ref[...], hf_ref[...], hi_ref[...], hg_ref[...],
                                [s0_ref[h].T for h in range(HG_HEADS)],
                                _hgrn_lb(lb_ref), g_ref[...], _tri(hq_ref.shape[0]))
    o_ref[...] = jnp.concatenate(outs, axis=1)
    for h in range(HG_HEADS):
        s_ref[h] = new_sts[h].T


def _hgrn_sample(z, state, hg_lb, g_out, t):
    nseq = state.shape[0]
    assert HG_CHUNK % t == 0 and t % SUBLANES == 0
    w = HG_HEADS * HG_DK
    col0 = (Q_LORA + KV_LORA) // w
    zspec = lambda group: pl.BlockSpec((t, w), lambda bi: (bi, col0 + group))
    sspec = pl.BlockSpec((None, HG_HEADS, HG_DK, HG_DV), lambda bi: (bi, 0, 0, 0))
    est = 2 * (5 * _nbytes((t, w), F32) + 2 * _nbytes((HG_HEADS, HG_DK, HG_DV), F32)) + 64 * _nbytes((HG_DK, HG_DV), F32)
    return pl.pallas_call(
        _hgrn_sample_kernel,
        grid=(nseq,),
        in_specs=[zspec(0), zspec(1), zspec(2), zspec(3), sspec,
                  pl.BlockSpec((2, w), lambda bi: (0, 0)),
                  pl.BlockSpec((1, HG_DV), lambda bi: (0, 0))],
        out_specs=[pl.BlockSpec((t, w), lambda bi: (bi, 0)), sspec],
        out_shape=[jax.ShapeDtypeStruct((nseq * t, w), F32),
                   jax.ShapeDtypeStruct(state.shape, F32)],
        compiler_params=pltpu.CompilerParams(
            dimension_semantics=("parallel",), vmem_limit_bytes=_vmem_limit(est)),
        name="hgrn_sample",
    )(z, z, z, z, state, hg_lb, g_out)


def _out_proj_kernel(x_ref, olat_ref, ohg_ref, wuv_ref, gm_ref, wo_ref, h_ref):
    parts = []
    for h in range(MLA_HEADS):
        om = _dot(olat_ref[:, h * KV_LORA:(h + 1) * KV_LORA], wuv_ref[h])
        parts.append(_rms(om, gm_ref[...]).astype(BF16))
    o_mla = jnp.concatenate(parts, axis=1)
    w = MLA_HEADS * V_HEAD
    h_ref[...] = (x_ref[...] + _dot(o_mla, wo_ref[:w, :])
                  + _dot(ohg_ref[...].astype(BF16), wo_ref[w:, :]))


def _out_proj(x, o_lat, o_hg, w_uvT, g_mla, w_o):
    n, d = x.shape
    tm = min(256, n)
    assert n % tm == 0
    lw = MLA_HEADS * KV_LORA
    hw = o_hg.shape[1]
    est = (2 * (2 * _nbytes((tm, d), F32) + _nbytes((tm, lw), BF16) + _nbytes((tm, hw), o_hg.dtype)
                + _nbytes(w_uvT.shape, BF16) + _nbytes(w_o.shape, BF16))
           + 4 * _nbytes((tm, d), F32))
    full = lambda shape: pl.BlockSpec(shape, lambda i: (0,) * len(shape))
    return pl.pallas_call(
        _out_proj_kernel,
        grid=(n // tm,),
        in_specs=[pl.BlockSpec((tm, d), lambda i: (i, 0)),
                  pl.BlockSpec((tm, lw), lambda i: (i, 0)),
                  pl.BlockSpec((tm, hw), lambda i: (i, 0)),
                  full(w_uvT.shape), full((1, V_HEAD)), full(w_o.shape)],
        out_specs=pl.BlockSpec((tm, d), lambda i: (i, 0)),
        out_shape=jax.ShapeDtypeStruct((n, d), F32),
        compiler_params=pltpu.CompilerParams(
            dimension_semantics=("parallel",), vmem_limit_bytes=_vmem_limit(est)),
        name="out_proj",
    )(x, o_lat, o_hg, w_uvT, g_mla, w_o)


def _ffn_kernel(h_ref, g_ref, wg_ref, wu_ref, wd_ref, o_ref, hn_scr):
    @pl.when(pl.program_id(1) == 0)
    def _():
        h = h_ref[...]
        hn_scr[...] = _rms(h, g_ref[...]).astype(BF16)
        o_ref[...] = h

    hn = hn_scr[...]
    gate = _dot(hn, wg_ref[...])
    up = _dot(hn, wu_ref[...])
    act = (gate * _sigmoid(gate) * up).astype(BF16)
    o_ref[...] += _dot(act, wd_ref[...])


def _ffn(h, g, w_gate, w_up, w_down):
    n, d = h.shape
    dff = w_gate.shape[1]
    tm = min(512, n)
    tf = 512
    assert n % tm == 0 and dff % tf == 0
    est = (2 * (2 * _nbytes((tm, d), F32) + 3 * _nbytes((d, tf), BF16))
           + _nbytes((tm, d), BF16) + 4 * _nbytes((tm, tf), F32) + _nbytes((tm, d), F32))
    return pl.pallas_call(
        _ffn_kernel,
        grid=(n // tm, dff // tf),
        in_specs=[pl.BlockSpec((tm, d), lambda i, j: (i, 0)),
                  pl.BlockSpec((1, d), lambda i, j: (0, 0)),
                  pl.BlockSpec((d, tf), lambda i, j: (0, j)),
                  pl.BlockSpec((d, tf), lambda i, j: (0, j)),
                  pl.BlockSpec((tf, d), lambda i, j: (j, 0))],
        out_specs=pl.BlockSpec((tm, d), lambda i, j: (i, 0)),
        out_shape=jax.ShapeDtypeStruct((n, d), F32),
        scratch_shapes=[pltpu.VMEM((tm, d), BF16)],
        compiler_params=pltpu.CompilerParams(
            dimension_semantics=("parallel", "arbitrary"), vmem_limit_bytes=_vmem_limit(est)),
        name="ffn",
    )(h, g, w_gate, w_up, w_down)


def _ple_final_kernel(h_ref, p_ref, gp_ref, wg_ref, wp_ref, gf_ref, y_ref):
    h = h_ref[...]
    a = _dot(_rms(h, gp_ref[...]).astype(BF16), wg_ref[...])
    pp = _dot(p_ref[...].astype(BF16), wp_ref[...])
    y_ref[...] = _rms(h + pp * _sigmoid(a), gf_ref[...])


def _ple_final(h, p, g_ple, w_gate, w_proj, g_final):
    n, d = h.shape
    pd = p.shape[1]
    tm = min(256, n)
    assert n % tm == 0
    est = (2 * (2 * _nbytes((tm, d), F32) + _nbytes((tm, pd), F32)
                + _nbytes(w_gate.shape, BF16) + _nbytes(w_proj.shape, BF16))
           + 4 * _nbytes((tm, d), F32))
    full = lambda shape: pl.BlockSpec(shape, lambda i: (0,) * len(shape))
    return pl.pallas_call(
        _ple_final_kernel,
        grid=(n // tm,),
        in_specs=[pl.BlockSpec((tm, d), lambda i: (i, 0)),
                  pl.BlockSpec((tm, pd), lambda i: (i, 0)),
                  full((1, d)), full(w_gate.shape), full(w_proj.shape), full((1, d))],
        out_specs=pl.BlockSpec((tm, d), lambda i: (i, 0)),
        out_shape=jax.ShapeDtypeStruct((n, d), F32),
        compiler_params=pltpu.CompilerParams(
            dimension_semantics=("parallel",), vmem_limit_bytes=_vmem_limit(est)),
        name="ple_final",
    )(h, p, g_ple, w_gate, w_proj, g_final)


def _rope_tables(pos, reps):
    inv = ROPE_BASE ** (-jnp.arange(0, QK_ROPE, 2, dtype=F32) / QK_ROPE)
    ang = pos.astype(F32)[:, None] * inv[None, :]
    cos, sin = jnp.cos(ang), jnp.sin(ang)
    cos128 = jnp.tile(jnp.concatenate([cos, cos], axis=1), (reps, LANES // QK_ROPE))
    sin128 = jnp.tile(jnp.concatenate([-sin, sin], axis=1), (reps, LANES // QK_ROPE))
    return cos128, sin128


def _prep_weights(g_mix, w_in, g_q, w_uq, g_kv, w_uk, w_uv, g_mla_out, g_hg_out, w_o, g_ffn,
                  w_ffn_gate, w_ffn_up, w_ffn_down, g_ple, w_ple_gate, w_ple_proj, g_final):
    row = lambda g: g.reshape(1, -1).astype(F32)
    half = QK_ROPE // 2
    lo = Q_LORA + KV_LORA
    w_main = jnp.concatenate([w_in[:, :lo], w_in[:, lo + QK_ROPE:]], axis=1).astype(BF16)
    w_k = w_in[:, lo:lo + QK_ROPE]
    lane_pad = lambda a: jnp.pad(a, [(0, 0)] * (a.ndim - 1) + [(0, LANES - a.shape[-1])])
    w_kpe = jnp.concatenate([lane_pad(w_k), lane_pad(jnp.roll(w_k, half, axis=1))], axis=1).astype(BF16)
    w_nope = w_uq[:, :, :QK_NOPE].reshape(Q_LORA, MLA_HEADS * QK_NOPE).astype(BF16)
    pe = w_uq[:, :, QK_NOPE:]
    w_pe = lane_pad(pe).reshape(Q_LORA, MLA_HEADS * LANES).astype(BF16)
    w_pe_rot = lane_pad(jnp.roll(pe, half, axis=2)).reshape(Q_LORA, MLA_HEADS * LANES).astype(BF16)
    w_ukT = jnp.transpose(w_uk, (1, 2, 0)).astype(BF16)
    w_uvT = jnp.transpose(w_uv, (1, 0, 2)).astype(BF16)
    return dict(
        g_mix=row(g_mix), w_main=w_main, w_kpe=w_kpe, g_q=row(g_q), g_kv=row(g_kv),
        w_nope=w_nope, w_pe=w_pe, w_pe_rot=w_pe_rot, w_ukT=w_ukT, w_uvT=w_uvT,
        g_mla=row(g_mla_out), g_hg=row(g_hg_out), w_o=w_o.astype(BF16), g_ffn=row(g_ffn),
        w_gate=w_ffn_gate.astype(BF16), w_up=w_ffn_up.astype(BF16), w_down=w_ffn_down.astype(BF16),
        g_ple=row(g_ple), w_ple_gate=w_ple_gate.astype(BF16), w_ple_proj=w_ple_proj.astype(BF16),
        g_final=row(g_final))


def _front(x, pos, w):
    b, t, d = x.shape
    cos128, sin128 = _rope_tables(pos, b)
    x2 = x.reshape(b * t, d)
    z, kpe, kpe_b = _in_proj(x2, w["g_mix"], cos128, sin128, w["w_main"], w["w_kpe"])
    q_cat, k_cat, ckv = _mla_q(z, kpe_b, cos128, sin128, w["g_q"], w["g_kv"],
                               w["w_nope"], w["w_pe"], w["w_pe_rot"], w["w_ukT"])
    return x2, z, q_cat, k_cat, ckv, kpe


def _back(x2, o_lat, o_hg, p, w):
    n = x2.shape[0]
    h = _out_proj(x2, o_lat, o_hg, w["w_uvT"], w["g_mla"], w["w_o"])
    h = _ffn(h, w["g_ffn"], w["w_gate"], w["w_up"], w["w_down"])
    return _ple_final(h, p.reshape(n, -1), w["g_ple"], w["w_ple_gate"], w["w_ple_proj"], w["g_final"])


def kernel(x_prompt, x_sample, cache_kv_latent, cache_k_rope, state_hgrn, page_table, p_prompt, p_sample, g_mix, w_in, g_q, w_uq, g_kv, w_uk, w_uv, g_mla_out, hg_lb, g_hg_out, w_o, g_ffn, w_ffn_gate, w_ffn_up, w_ffn_down, g_ple, w_ple_gate, w_ple_proj, g_final):
    depth = g_mix.shape[0]
    assert depth == 1 and hg_lb.shape[0] == 2, "single-layer trunk only"
    w = _prep_weights(g_mix[0], w_in[0], g_q[0], w_uq[0], g_kv[0], w_uk[0], w_uv[0], g_mla_out[0],
                      g_hg_out[0], w_o[0], g_ffn[0], w_ffn_gate[0], w_ffn_up[0], w_ffn_down[0],
                      g_ple[0], w_ple_gate[0], w_ple_proj[0], g_final)
    hg_lb = hg_lb.astype(F32)
    bp, tp, d = x_prompt.shape
    bs, ts, _ = x_sample.shape
    past_len = page_table.shape[1] * PAGE_SIZE

    x2, z, q_cat, k_cat, ckv, kpe = _front(x_prompt, jnp.arange(tp, dtype=jnp.int32), w)
    o_lat = _attn_prompt(q_cat, k_cat, bp, tp)
    o_hg, st_p = _hgrn_prompt(z, hg_lb, w["g_hg"], bp, tp)
    y_prompt = _back(x2, o_lat, o_hg, p_prompt, w).reshape(bp, tp, d)
    lat_p = ckv.reshape(1, bp, tp, KV_LORA)
    rope_p = kpe.reshape(1, bp, tp, QK_ROPE)

    pos_s = past_len + jnp.arange(ts, dtype=jnp.int32)
    x2, z, q_cat, k_cat, ckv, kpe = _front(x_sample, pos_s, w)
    n_pool = cache_kv_latent.shape[1]
    o_lat = _attn_sample(page_table, q_cat.reshape(bs, ts * MLA_HEADS, QK_CAT),
                         ckv.reshape(bs, ts, KV_LORA), kpe.reshape(bs, ts, QK_ROPE),
                         cache_kv_latent.reshape(n_pool, PAGE_SIZE, KV_LORA),
                         jnp.swapaxes(cache_k_rope.reshape(n_pool, PAGE_SIZE, QK_ROPE), 1, 2))
    o_hg, st_s = _hgrn_sample(z, state_hgrn.reshape(state_hgrn.shape[1:]), hg_lb, w["g_hg"], ts)
    y_sample = _back(x2, o_lat.reshape(bs * ts, MLA_HEADS * KV_LORA), o_hg, p_sample, w).reshape(bs, ts, d)
    lat_s = ckv.reshape(1, bs, ts, KV_LORA)
    rope_s = kpe.reshape(1, bs, ts, QK_ROPE)

    return (y_prompt, y_sample, lat_p, rope_p, st_p[None], lat_s, rope_s, st_s[None])
```

```python
import functools
import math

import jax
import jax.numpy as jnp
from jax import lax
from jax.experimental import pallas as pl
from jax.experimental.pallas import tpu as pltpu

F32 = jnp.float32
BF16 = jnp.bfloat16

EPS = 1e-6
ROPE_BASE = 10000.0
PAGE_SIZE = 128
MLA_HEADS = 8
V_HEAD = 128
QK_NOPE = 128
QK_ROPE = 64
Q_LORA = 512
KV_LORA = 512
ATTN_SCALE = (QK_NOPE + QK_ROPE) ** -0.5
HG_HEADS = 8
HG_DK = 128
HG_DV = 128
HG_CHUNK = 64
HG_SUB = 16
LOG2E = math.log2(math.e)

V7X_VMEM_BYTES = 64 * 1024 * 1024
LANES = 128
SUBLANES = 8

QK_CAT = KV_LORA + LANES

_NT = (((1,), (1,)), ((), ()))
_TN = (((0,), (0,)), ((), ()))


def _vmem_limit(nbytes):
    return int(min(max(nbytes, 16 * 1024 * 1024), V7X_VMEM_BYTES - 8 * 1024 * 1024))


def _nbytes(shape, dtype):
    return math.prod(shape) * jnp.dtype(dtype).itemsize


def _rms(x, g):
    ms = jnp.mean(x * x, axis=-1, keepdims=True)
    return x * lax.rsqrt(ms + EPS) * g


def _dot(a, b):
    return jnp.dot(a, b, preferred_element_type=F32)


def _sigmoid(x):
    return 1.0 / (1.0 + jnp.exp(-x))


def _in_proj_kernel(x_ref, g_ref, cos_ref, sin_ref, wm_ref, wk_ref,
                    z_ref, kpe_ref, kpeb_ref, u_scr):
    @pl.when(pl.program_id(1) == 0)
    def _():
        u = _rms(x_ref[...], g_ref[...]).astype(BF16)
        u_scr[...] = u
        kk = _dot(u, wk_ref[...])
        kpe = kk[:, :LANES] * cos_ref[...] + kk[:, LANES:] * sin_ref[...]
        kpe_ref[...] = kpe[:, :QK_ROPE]
        kpeb_ref[...] = kpe.astype(BF16)

    z_ref[...] = _dot(u_scr[...], wm_ref[...])


def _in_proj(x, g, cos128, sin128, w_main, w_kpe):
    n, d = x.shape
    nout = w_main.shape[1]
    tm = min(1024, n)
    tn = 1024
    assert n % tm == 0 and nout % tn == 0
    est = (2 * (_nbytes((tm, d), F32) + _nbytes((d, tn), BF16) + _nbytes((tm, tn), F32))
           + 2 * _nbytes((tm, d), BF16) + _nbytes((tm, d), F32))
    return pl.pallas_call(
        _in_proj_kernel,
        grid=(n // tm, nout // tn),
        in_specs=[
            pl.BlockSpec((tm, d), lambda i, j: (i, 0)),
            pl.BlockSpec((1, d), lambda i, j: (0, 0)),
            pl.BlockSpec((tm, LANES), lambda i, j: (i, 0)),
            pl.BlockSpec((tm, LANES), lambda i, j: (i, 0)),
            pl.BlockSpec((d, tn), lambda i, j: (0, j)),
            pl.BlockSpec((d, 2 * LANES), lambda i, j: (0, 0)),
        ],
        out_specs=[
            pl.BlockSpec((tm, tn), lambda i, j: (i, j)),
            pl.BlockSpec((tm, QK_ROPE), lambda i, j: (i, 0)),
            pl.BlockSpec((tm, LANES), lambda i, j: (i, 0)),
        ],
        out_shape=[
            jax.ShapeDtypeStruct((n, nout), F32),
            jax.ShapeDtypeStruct((n, QK_ROPE), F32),
            jax.ShapeDtypeStruct((n, LANES), BF16),
        ],
        scratch_shapes=[pltpu.VMEM((tm, d), BF16)],
        compiler_params=pltpu.CompilerParams(
            dimension_semantics=("parallel", "arbitrary"),
            vmem_limit_bytes=_vmem_limit(est)),
        name="in_proj",
    )(x, g, cos128, sin128, w_main, w_kpe)


def _mla_q_kernel(cq_ref, ckv_ref, kpeb_ref, cos_ref, sin_ref, gq_ref, gkv_ref,
                  wn_ref, wp_ref, wpr_ref, wuk_ref,
                  qcat_ref, kcat_ref, ckv_out_ref):
    cqn = _rms(cq_ref[...], gq_ref[...]).astype(BF16)
    qn = _dot(cqn, wn_ref[...]).astype(BF16)
    x = _dot(cqn, wp_ref[...])
    xr = _dot(cqn, wpr_ref[...])
    c, s = cos_ref[...], sin_ref[...]
    for h in range(MLA_HEADS):
        col = h * QK_CAT
        qa = _dot(qn[:, h * QK_NOPE:(h + 1) * QK_NOPE], wuk_ref[h])
        qcat_ref[:, col:col + KV_LORA] = (qa * ATTN_SCALE).astype(BF16)
        lanes = slice(h * LANES, (h + 1) * LANES)
        qcat_ref[:, col + KV_LORA:col + QK_CAT] = (
            (x[:, lanes] * c + xr[:, lanes] * s) * ATTN_SCALE).astype(BF16)
    ckv = _rms(ckv_ref[...], gkv_ref[...])
    ckv_out_ref[...] = ckv
    kcat_ref[:, :KV_LORA] = ckv.astype(BF16)
    kcat_ref[:, KV_LORA:] = kpeb_ref[...]


def _mla_q(z, kpe_b, cos128, sin128, g_q, g_kv, w_nope, w_pe, w_pe_rot, w_ukT):
    n = z.shape[0]
    tm = min(512, n)
    assert n % tm == 0
    qw = MLA_HEADS * QK_CAT
    est = (2 * (2 * _nbytes((tm, Q_LORA), F32) + 2 * _nbytes((tm, LANES), F32) + _nbytes((tm, LANES), BF16)
                + _nbytes((tm, qw), BF16) + _nbytes((tm, QK_CAT), BF16) + _nbytes((tm, KV_LORA), F32)
                + 3 * _nbytes(w_nope.shape, BF16) + _nbytes(w_ukT.shape, BF16))
           + 6 * _nbytes((tm, MLA_HEADS * QK_NOPE), F32))
    full = lambda shape: pl.BlockSpec(shape, lambda i: (0,) * len(shape))
    return pl.pallas_call(
        _mla_q_kernel,
        grid=(n // tm,),
        in_specs=[
            pl.BlockSpec((tm, Q_LORA), lambda i: (i, 0)),
            pl.BlockSpec((tm, KV_LORA), lambda i: (i, 1)),
            pl.BlockSpec((tm, LANES), lambda i: (i, 0)),
            pl.BlockSpec((tm, LANES), lambda i: (i, 0)),
            pl.BlockSpec((tm, LANES), lambda i: (i, 0)),
            full((1, Q_LORA)), full((1, KV_LORA)),
            full(w_nope.shape), full(w_pe.shape), full(w_pe_rot.shape), full(w_ukT.shape),
        ],
        out_specs=[
            pl.BlockSpec((tm, qw), lambda i: (i, 0)),
            pl.BlockSpec((tm, QK_CAT), lambda i: (i, 0)),
            pl.BlockSpec((tm, KV_LORA), lambda i: (i, 0)),
        ],
        out_shape=[
            jax.ShapeDtypeStruct((n, qw), BF16),
            jax.ShapeDtypeStruct((n, QK_CAT), BF16),
            jax.ShapeDtypeStruct((n, KV_LORA), F32),
        ],
        compiler_params=pltpu.CompilerParams(
            dimension_semantics=("parallel",), vmem_limit_bytes=_vmem_limit(est)),
        name="mla_q",
    )(z, z, kpe_b, cos128, sin128, g_q, g_kv, w_nope, w_pe, w_pe_rot, w_ukT)


def _lane_tile(x, width):
    reps = width // LANES
    return x if reps == 1 else jnp.concatenate([x] * reps, axis=1)


def _softmax_update(s, v, m_scr, l_scr, acc_scr):
    m_prev = m_scr[...]
    m_new = jnp.maximum(m_prev, jnp.max(s, axis=1, keepdims=True))
    alpha = jnp.exp(m_prev - m_new)
    p = jnp.exp(s - _lane_tile(m_new, s.shape[1]))
    l_scr[...] = alpha * l_scr[...] + jnp.sum(p, axis=1, keepdims=True)
    m_scr[...] = m_new
    acc_scr[...] = (acc_scr[...] * _lane_tile(alpha, acc_scr.shape[1])
                    + _dot(p.astype(BF16), v))


def _softmax_init(m_scr, l_scr, acc_scr):
    m_scr[...] = jnp.full(m_scr.shape, -jnp.inf, F32)
    l_scr[...] = jnp.zeros(l_scr.shape, F32)
    acc_scr[...] = jnp.zeros(acc_scr.shape, F32)


def _softmax_result(l_scr, acc_scr):
    return acc_scr[...] / _lane_tile(l_scr[...], acc_scr.shape[1])


def _attn_kernel(pt_ref, q_ref, cn_ref, kn_ref, lat_hbm, rope_hbm, pq_ref, pk_ref,
                 o_ref, po_ref,
                 lat_buf, rope_buf, sems, latb_scr, ropeb_scr, m_scr, l_scr, acc_scr,
                 qs_scr, pm_scr, pl_scr, pacc_scr,
                 *, pages, group, n_chunks, tq, tk, nq):
    b = pl.program_id(0)
    rows, t_new = q_ref.shape[0], cn_ref.shape[0]

    half = lax.rem(b, 2)
    qi = lax.rem(b // 2, nq)
    n_full = (qi * tq) // tk
    max_full = ((nq - 1) * tq) // tk
    prows = tq * MLA_HEADS

    def prompt_scores(start):
        return lax.dot_general(qs_scr[...], pk_ref[pl.ds(start, tk), :], _NT,
                               preferred_element_type=F32)

    def prompt_begin():
        for h in range(MLA_HEADS):
            qs_scr[h * tq:(h + 1) * tq, :] = pq_ref[:, h * QK_CAT:(h + 1) * QK_CAT]
        _softmax_init(pm_scr, pl_scr, pacc_scr)

    def prompt_full(c):
        _softmax_update(prompt_scores(c * tk), pk_ref[c * tk:(c + 1) * tk, :KV_LORA],
                        pm_scr, pl_scr, pacc_scr)

    def prompt_diag():
        start = pl.multiple_of(n_full * tk, tk)
        row_tok = qi * tq + jnp.bitwise_and(lax.broadcasted_iota(jnp.int32, (prows, tk), 0), tq - 1)
        col = start + lax.broadcasted_iota(jnp.int32, (prows, tk), 1)
        s = jnp.where(col <= row_tok, prompt_scores(start), -jnp.inf)
        _softmax_update(s, pk_ref[pl.ds(start, tk), :KV_LORA], pm_scr, pl_scr, pacc_scr)
        res = _softmax_result(pl_scr, pacc_scr).astype(po_ref.dtype)
        for h in range(MLA_HEADS):
            po_ref[:, h * KV_LORA:(h + 1) * KV_LORA] = res[h * tq:(h + 1) * tq, :]

    def prompt_phase(p):
        if p == 0:
            pl.when(half == 0)(prompt_begin)
            pl.when(half == 1)(prompt_diag)
        if p < max_full:
            pl.when(jnp.logical_and(half == 0, p < n_full))(functools.partial(prompt_full, p))

    streams = ((lat_hbm, lat_buf), (rope_hbm, rope_buf))

    def page_copy(stream, page, slot, k):
        hbm, buf = streams[stream]
        return pltpu.make_async_copy(hbm.at[page], buf.at[slot, k], sems.at[stream, slot])

    def start_chunk(g, slot):
        for k in range(pages):
            page = pt_ref[g * pages + k]
            for stream in range(2):
                page_copy(stream, page, slot, k).start()

    def wait_chunk(slot):
        for stream in range(2):
            for k in range(pages):
                page_copy(stream, 0, slot, k).wait()

    @pl.when(b == 0)
    def _():
        start_chunk(0, 0)

    _softmax_init(m_scr, l_scr, acc_scr)

    pending = None
    groups_per_chunk = pages // group
    for gi in range(n_chunks * groups_per_chunk):
        c, j = divmod(gi, groups_per_chunk)
        slot = c % 2
        if j == 0:
            start_chunk(b * n_chunks + c + 1, 1 - slot)
            wait_chunk(slot)
        if gi == 0:
            pad = LANES - t_new
            cn = jnp.concatenate([cn_ref[...], jnp.zeros((pad, KV_LORA), F32)], axis=0).astype(BF16)
            kn = jnp.concatenate([kn_ref[...], jnp.zeros((pad, QK_ROPE), F32)], axis=0).astype(BF16)
            sn = lax.dot_general(q_ref[:, :KV_LORA], cn, _NT, preferred_element_type=F32)
            sn = sn + lax.dot_general(q_ref[:, KV_LORA:KV_LORA + QK_ROPE], kn, _NT,
                                      preferred_element_type=F32)
            row_tok = lax.shift_right_logical(lax.broadcasted_iota(jnp.int32, (rows, LANES), 0), 3)
            col = lax.broadcasted_iota(jnp.int32, (rows, LANES), 1)
            pending = (jnp.where(col <= row_tok, sn, -jnp.inf), lambda cn=cn: cn)
        for k in range(j * group, (j + 1) * group):
            latb_scr[k * PAGE_SIZE:(k + 1) * PAGE_SIZE, :] = lat_buf[slot, k].astype(BF16)
            ropeb_scr[:, k * PAGE_SIZE:(k + 1) * PAGE_SIZE] = rope_buf[slot, k].astype(BF16)
        keys = slice(j * group * PAGE_SIZE, (j + 1) * group * PAGE_SIZE)
        s = lax.dot_general(q_ref[:, :KV_LORA], latb_scr[keys, :], _NT, preferred_element_type=F32)
        s = s + _dot(q_ref[:, KV_LORA:KV_LORA + QK_ROPE], ropeb_scr[:, keys])
        if pending is not None:
            _softmax_update(pending[0], pending[1](), m_scr, l_scr, acc_scr)
        pending = (s, lambda keys=keys: latb_scr[keys, :])
        if j == groups_per_chunk - 1:
            _softmax_update(pending[0], pending[1](), m_scr, l_scr, acc_scr)
            pending = None
            if c == n_chunks - 1:
                o_ref[...] = _softmax_result(l_scr, acc_scr).astype(o_ref.dtype)
            prompt_phase(c)

    @pl.when(b == pl.num_programs(0) - 1)
    def _():
        wait_chunk(n_chunks % 2)


def _attention(page_table, q_cat, ckv_new, kpe_new, cache_lat, cache_rope_t, q_cat_p, k_cat_p, bp, tp):
    s_, rows, _ = q_cat.shape
    t_new = ckv_new.shape[1]
    tq = min(128, tp)
    tk = min(512, tp)
    nq = tp // tq
    prows = tq * MLA_HEADS
    qw, ow = MLA_HEADS * QK_CAT, MLA_HEADS * KV_LORA
    assert tp % tk == 0 and tk % tq == 0 and tq & (tq - 1) == 0
    assert s_ == 2 * bp * nq, "each prompt query block is spread over two sample sequences"
    n_pages = page_table.shape[1]
    pages = math.gcd(n_pages, 32)
    group = math.gcd(pages, 8)
    n_chunks = n_pages // pages
    assert MLA_HEADS == 8 and rows == t_new * MLA_HEADS and t_new <= LANES
    assert n_chunks % 2 == 0, "buffer slot parity must carry across sequences"
    assert pages // group >= 2, "a group's bf16 keys must outlive the next group's conversion"
    assert (tp - tq) // tk <= n_chunks, "one pre-diagonal prompt chunk per cache chunk"
    pt_flat = page_table.reshape(-1)
    pt_flat = jnp.concatenate([pt_flat, pt_flat[:pages]])
    keys = pages * PAGE_SIZE
    est = (2 * pages * (_nbytes((PAGE_SIZE, KV_LORA), F32) + _nbytes((QK_ROPE, PAGE_SIZE), F32))
           + _nbytes((keys, KV_LORA), BF16) + _nbytes((QK_ROPE, keys), BF16)
           + 8 * _nbytes((rows, group * PAGE_SIZE), F32) + 8 * _nbytes((rows, KV_LORA), F32)
           + 4 * _nbytes((rows, QK_CAT), BF16)
           + 2 * (_nbytes((tq, qw), BF16) + _nbytes((tp, QK_CAT), BF16) + _nbytes((tq, ow), BF16))
           + _nbytes((prows, QK_CAT), BF16) + 2 * _nbytes((prows, LANES), F32)
           + _nbytes((prows, KV_LORA), F32) + 5 * _nbytes((prows, tk), F32))
    grid_spec = pltpu.PrefetchScalarGridSpec(
        num_scalar_prefetch=1,
        grid=(s_,),
        in_specs=[
            pl.BlockSpec((None, rows, QK_CAT), lambda b, pt: (b, 0, 0)),
            pl.BlockSpec((None, t_new, KV_LORA), lambda b, pt: (b, 0, 0)),
            pl.BlockSpec((None, t_new, QK_ROPE), lambda b, pt: (b, 0, 0)),
            pl.BlockSpec(memory_space=pl.ANY),
            pl.BlockSpec(memory_space=pl.ANY),
            pl.BlockSpec((tq, qw), lambda b, pt: (b // 2, 0)),
            pl.BlockSpec((tp, QK_CAT), lambda b, pt: (b // (2 * nq), 0)),
        ],
        out_specs=[
            pl.BlockSpec((None, rows, KV_LORA), lambda b, pt: (b, 0, 0)),
            pl.BlockSpec((tq, ow), lambda b, pt: (b // 2, 0)),
        ],
        scratch_shapes=[
            pltpu.VMEM((2, pages, PAGE_SIZE, KV_LORA), F32),
            pltpu.VMEM((2, pages, QK_ROPE, PAGE_SIZE), F32),
            pltpu.SemaphoreType.DMA((2, 2)),
            pltpu.VMEM((keys, KV_LORA), BF16), pltpu.VMEM((QK_ROPE, keys), BF16),
            pltpu.VMEM((rows, LANES), F32), pltpu.VMEM((rows, LANES), F32),
            pltpu.VMEM((rows, KV_LORA), F32),
            pltpu.VMEM((prows, QK_CAT), BF16),
            pltpu.VMEM((prows, LANES), F32), pltpu.VMEM((prows, LANES), F32),
            pltpu.VMEM((prows, KV_LORA), F32)],
    )
    return pl.pallas_call(
        functools.partial(_attn_kernel, pages=pages, group=group, n_chunks=n_chunks,
                          tq=tq, tk=tk, nq=nq),
        grid_spec=grid_spec,
        out_shape=[jax.ShapeDtypeStruct((s_, rows, KV_LORA), BF16),
                   jax.ShapeDtypeStruct((bp * tp, ow), BF16)],
        compiler_params=pltpu.CompilerParams(
            dimension_semantics=("arbitrary",), vmem_limit_bytes=_vmem_limit(est)),
        name="attention",
    )(pt_flat, q_cat, ckv_new, kpe_new, cache_lat, cache_rope_t, q_cat_p, k_cat_p)


def _hgrn_chunk(hq, hf, v, hg, sts, lb, g_out, tri):
    nh = len(sts)
    c = hq.shape[0]
    sub = min(HG_SUB, c)
    n_sub = c // sub
    n_rg = sub // SUBLANES
    hs = [slice(h * HG_DK, (h + 1) * HG_DK) for h in range(nh)]
    q = hq * _sigmoid(hq)
    log_f = jnp.log(lb + (1.0 - lb) * _sigmoid(hf))
    k = (1.0 - lb) * _sigmoid(-hf)
    b = jnp.dot(tri, log_f, preferred_element_type=F32, precision=lax.Precision.HIGHEST)
    b2 = b * LOG2E
    vb = v.astype(BF16)
    row = lax.broadcasted_iota(jnp.int32, (SUBLANES, hq.shape[1]), 0)
    o_sub = []
    for i in range(n_sub):
        lo = i * sub
        acc = [[jnp.zeros((SUBLANES, HG_DV), F32) for _ in range(n_rg)] for _ in range(nh)]
        for s in range(sub):
            ks, bs, vs = k[lo + s:lo + s + 1], b2[lo + s:lo + s + 1], v[lo + s:lo + s + 1]
            for rg in range(s // SUBLANES, n_rg):
                r0 = lo + rg * SUBLANES
                e = jnp.exp2(b2[r0:r0 + SUBLANES] - bs)
                if rg == s // SUBLANES:
                    e = jnp.where(row >= s % SUBLANES, e, 0.0)
                prod = q[r0:r0 + SUBLANES] * ks * e
                for h in range(nh):
                    a_col = jnp.sum(prod[:, hs[h]], axis=1, keepdims=True)
                    acc[h][rg] = acc[h][rg] + a_col * vs[:, hs[h]]
        o_i = [a[0] if n_rg == 1 else jnp.concatenate(a, axis=0) for a in acc]
        if i > 0:
            r = b[lo - 1:lo]
            qh = (q[lo:lo + sub] * jnp.exp(b[lo:lo + sub] - r)).astype(BF16)
            kh = (k[:lo] * jnp.exp(r - b[:lo])).astype(BF16)
            for h in range(nh):
                a_off = lax.dot_general(qh[:, hs[h]], kh[:, hs[h]], _NT,
                                        preferred_element_type=F32)
                o_i[h] = o_i[h] + _dot(a_off.astype(BF16), vb[:lo, hs[h]])
        o_sub.append(o_i)
    qe = (q * jnp.exp(b)).astype(BF16)
    b_last = b[c - 1:c]
    kd = (k * jnp.exp(b_last - b)).astype(BF16)
    decay = jnp.exp(b_last)
    gate = hg * _sigmoid(hg)
    outs, new_sts = [], []
    for h in range(nh):
        o = o_sub[0][h] if n_sub == 1 else jnp.concatenate([p[h] for p in o_sub], axis=0)
        o = o + lax.dot_general(qe[:, hs[h]], sts[h].astype(BF16), _NT, preferred_element_type=F32)
        new_sts.append(sts[h] * decay[:, hs[h]]
                       + lax.dot_general(vb[:, hs[h]], kd[:, hs[h]], _TN, preferred_element_type=F32))
        outs.append(_rms(o, g_out) * gate[:, hs[h]])
    return outs, new_sts


def _hgrn_lb(lb_ref):
    a = lb_ref[...]
    a0, a1 = a[0:1], a[1:2]
    m = jnp.maximum(a0, a1)
    e0, e1 = jnp.exp(a0 - m), jnp.exp(a1 - m)
    return e0 / (e0 + e1)


def _tri(c):
    return (lax.broadcasted_iota(jnp.int32, (c, c), 0)
            >= lax.broadcasted_iota(jnp.int32, (c, c), 1)).astype(F32)


def _hgrn_prompt_kernel(hq_ref, hf_ref, hi_ref, hg_ref, lb_ref, g_ref, o_ref, s_ref, st_scr,
                        *, chunk, n_chunks, heads):
    step = pl.program_id(2)

    @pl.when(step == 0)
    def _():
        st_scr[...] = jnp.zeros(st_scr.shape, F32)

    lb = _hgrn_lb(lb_ref)
    g = g_ref[...]
    tri = _tri(chunk)

    def body(ci, carry):
        rows = pl.ds(pl.multiple_of(ci * chunk, chunk), chunk)
        outs, new_sts = _hgrn_chunk(hq_ref[rows, :], hf_ref[rows, :], hi_ref[rows, :], hg_ref[rows, :],
                                    [st_scr[h] for h in range(heads)], lb, g, tri)
        o_ref[rows, :] = jnp.concatenate(outs, axis=1).astype(o_ref.dtype)
        for h in range(heads):
            st_scr[h] = new_sts[h]
        return carry

    lax.fori_loop(0, n_chunks, body, 0)

    @pl.when(step == pl.num_programs(2) - 1)
    def _():
        for h in range(heads):
            s_ref[h] = st_scr[h].T


def _hgrn_prompt(z, hg_lb, g_out, b, t):
    chunk = math.gcd(t, HG_CHUNK)
    n_chunks = min(4, t // chunk)
    tb = chunk * n_chunks
    heads = 8
    assert t % tb == 0 and HG_HEADS % heads == 0
    steps = t // tb
    w = heads * HG_DK
    col0 = (Q_LORA + KV_LORA) // w
    groups = HG_HEADS // heads

    def zspec(group):
        return pl.BlockSpec((tb, w), lambda bi, h, s: (bi * steps + s, col0 + group * groups + h))

    est = (2 * (4 * _nbytes((tb, w), F32) + _nbytes((tb, w), BF16))
           + (3 + 64) * heads * _nbytes((HG_DK, HG_DV), F32))
    return pl.pallas_call(
        functools.partial(_hgrn_prompt_kernel, chunk=chunk, n_chunks=n_chunks, heads=heads),
        grid=(b, groups, steps),
        in_specs=[zspec(0), zspec(1), zspec(2), zspec(3),
                  pl.BlockSpec((2, w), lambda bi, h, s: (0, h)),
                  pl.BlockSpec((1, HG_DV), lambda bi, h, s: (0, 0))],
        out_specs=[pl.BlockSpec((tb, w), lambda bi, h, s: (bi * steps + s, h)),
                   pl.BlockSpec((None, heads, HG_DK, HG_DV), lambda bi, h, s: (bi, h, 0, 0))],
        out_shape=[jax.ShapeDtypeStruct((b * t, HG_HEADS * HG_DV), BF16),
                   jax.ShapeDtypeStruct((b, HG_HEADS, HG_DK, HG_DV), F32)],
        scratch_shapes=[pltpu.VMEM((heads, HG_DV, HG_DK), F32)],
        compiler_params=pltpu.CompilerParams(
            dimension_semantics=("parallel", "parallel", "arbitrary"),
            vmem_limit_bytes=_vmem_limit(est)),
        name="hgrn_prompt",
    )(z, z, z, z, hg_lb, g_out)


def _hgrn_sample_kernel(hq_ref, hf_ref, hi_ref, hg_ref, s0_ref, lb_ref, g_ref, o_ref, s_ref):
    outs, new_sts = _hgrn_chunk(hq_ref[...], hf_ref[...], hi_ref[...], hg_ref[...],
                                [s0_ref[h].T for h in range(HG_HEADS)],
                                _hgrn_lb(lb_ref), g_ref[...], _tri(hq_ref.shape[0]))
    o_ref[...] = jnp.concatenate(outs, axis=1)
    for h in range(HG_HEADS):
        s_ref[h] = new_sts[h].T


def _hgrn_sample(z, state, hg_lb, g_out, t):
    nseq = state.shape[0]
    assert HG_CHUNK % t == 0 and t % SUBLANES == 0
    w = HG_HEADS * HG_DK
    col0 = (Q_LORA + KV_LORA) // w
    zspec = lambda group: pl.BlockSpec((t, w), lambda bi: (bi, col0 + group))
    sspec = pl.BlockSpec((None, HG_HEADS, HG_DK, HG_DV), lambda bi: (bi, 0, 0, 0))
    est = 2 * (5 * _nbytes((t, w), F32) + 2 * _nbytes((HG_HEADS, HG_DK, HG_DV), F32)) + 64 * _nbytes((HG_DK, HG_DV), F32)
    return pl.pallas_call(
        _hgrn_sample_kernel,
        grid=(nseq,),
        in_specs=[zspec(0), zspec(1), zspec(2), zspec(3), sspec,
                  pl.BlockSpec((2, w), lambda bi: (0, 0)),
                  pl.BlockSpec((1, HG_DV), lambda bi: (0, 0))],
        out_specs=[pl.BlockSpec((t, w), lambda bi: (bi, 0)), sspec],
        out_shape=[jax.ShapeDtypeStruct((nseq * t, w), F32),
                   jax.ShapeDtypeStruct(state.shape, F32)],
        compiler_params=pltpu.CompilerParams(
            dimension_semantics=("parallel",), vmem_limit_bytes=_vmem_limit(est)),
        name="hgrn_sample",
    )(z, z, z, z, state, hg_lb, g_out)


def _out_proj_kernel(x_ref, olat_ref, ohg_ref, wuv_ref, gm_ref, wo_ref, h_ref):
    parts = []
    for h in range(MLA_HEADS):
        om = _dot(olat_ref[:, h * KV_LORA:(h + 1) * KV_LORA], wuv_ref[h])
        parts.append(_rms(om, gm_ref[...]).astype(BF16))
    o_mla = jnp.concatenate(parts, axis=1)
    w = MLA_HEADS * V_HEAD
    h_ref[...] = (x_ref[...] + _dot(o_mla, wo_ref[:w, :])
                  + _dot(ohg_ref[...].astype(BF16), wo_ref[w:, :]))


def _out_proj(x, o_lat, o_hg, w_uvT, g_mla, w_o):
    n, d = x.shape
    tm = min(256, n)
    assert n % tm == 0
    lw = MLA_HEADS * KV_LORA
    hw = o_hg.shape[1]
    est = (2 * (2 * _nbytes((tm, d), F32) + _nbytes((tm, lw), BF16) + _nbytes((tm, hw), o_hg.dtype)
                + _nbytes(w_uvT.shape, BF16) + _nbytes(w_o.shape, BF16))
           + 4 * _nbytes((tm, d), F32))
    full = lambda shape: pl.BlockSpec(shape, lambda i: (0,) * len(shape))
    return pl.pallas_call(
        _out_proj_kernel,
        grid=(n // tm,),
        in_specs=[pl.BlockSpec((tm, d), lambda i: (i, 0)),
                  pl.BlockSpec((tm, lw), lambda i: (i, 0)),
                  pl.BlockSpec((tm, hw), lambda i: (i, 0)),
                  full(w_uvT.shape), full((1, V_HEAD)), full(w_o.shape)],
        out_specs=pl.BlockSpec((tm, d), lambda i: (i, 0)),
        out_shape=jax.ShapeDtypeStruct((n, d), F32),
        compiler_params=pltpu.CompilerParams(
            dimension_semantics=("parallel",), vmem_limit_bytes=_vmem_limit(est)),
        name="out_proj",
    )(x, o_lat, o_hg, w_uvT, g_mla, w_o)


def _ffn_kernel(h_ref, g_ref, wg_ref, wu_ref, wd_ref, o_ref, hn_scr):
    @pl.when(pl.program_id(1) == 0)
    def _():
        h = h_ref[...]
        hn_scr[...] = _rms(h, g_ref[...]).astype(BF16)
        o_ref[...] = h

    hn = hn_scr[...]
    gate = _dot(hn, wg_ref[...])
    up = _dot(hn, wu_ref[...])
    act = (gate * _sigmoid(gate) * up).astype(BF16)
    o_ref[...] += _dot(act, wd_ref[...])


def _ffn(h, g, w_gate, w_up, w_down):
    n, d = h.shape
    dff = w_gate.shape[1]
    tm = min(512, n)
    tf = 512
    assert n % tm == 0 and dff % tf == 0
    est = (2 * (2 * _nbytes((tm, d), F32) + 3 * _nbytes((d, tf), BF16))
           + _nbytes((tm, d), BF16) + 4 * _nbytes((tm, tf), F32) + _nbytes((tm, d), F32))
    return pl.pallas_call(
        _ffn_kernel,
        grid=(n // tm, dff // tf),
        in_specs=[pl.BlockSpec((tm, d), lambda i, j: (i, 0)),
                  pl.BlockSpec((1, d), lambda i, j: (0, 0)),
                  pl.BlockSpec((d, tf), lambda i, j: (0, j)),
                  pl.BlockSpec((d, tf), lambda i, j: (0, j)),
                  pl.BlockSpec((tf, d), lambda i, j: (j, 0))],
        out_specs=pl.BlockSpec((tm, d), lambda i, j: (i, 0)),
        out_shape=jax.ShapeDtypeStruct((n, d), F32),
        scratch_shapes=[pltpu.VMEM((tm, d), BF16)],
        compiler_params=pltpu.CompilerParams(
            dimension_semantics=("parallel", "arbitrary"), vmem_limit_bytes=_vmem_limit(est)),
        name="ffn",
    )(h, g, w_gate, w_up, w_down)


def _ple_final_kernel(h_ref, p_ref, gp_ref, wg_ref, wp_ref, gf_ref, y_ref):
    h = h_ref[...]
    a = _dot(_rms(h, gp_ref[...]).astype(BF16), wg_ref[...])
    pp = _dot(p_ref[...].astype(BF16), wp_ref[...])
    y_ref[...] = _rms(h + pp * _sigmoid(a), gf_ref[...])


def _ple_final(h, p, g_ple, w_gate, w_proj, g_final):
    n, d = h.shape
    pd = p.shape[1]
    tm = min(256, n)
    assert n % tm == 0
    est = (2 * (2 * _nbytes((tm, d), F32) + _nbytes((tm, pd), F32)
                + _nbytes(w_gate.shape, BF16) + _nbytes(w_proj.shape, BF16))
           + 4 * _nbytes((tm, d), F32))
    full = lambda shape: pl.BlockSpec(shape, lambda i: (0,) * len(shape))
    return pl.pallas_call(
        _ple_final_kernel,
        grid=(n // tm,),
        in_specs=[pl.BlockSpec((tm, d), lambda i: (i, 0)),
                  pl.BlockSpec((tm, pd), lambda i: (i, 0)),
                  full((1, d)), full(w_gate.shape), full(w_proj.shape), full((1, d))],
        out_specs=pl.BlockSpec((tm, d), lambda i: (i, 0)),
        out_shape=jax.ShapeDtypeStruct((n, d), F32),
        compiler_params=pltpu.CompilerParams(
            dimension_semantics=("parallel",), vmem_limit_bytes=_vmem_limit(est)),
        name="ple_final",
    )(h, p, g_ple, w_gate, w_proj, g_final)


def _rope_tables(pos, reps):
    inv = ROPE_BASE ** (-jnp.arange(0, QK_ROPE, 2, dtype=F32) / QK_ROPE)
    ang = pos.astype(F32)[:, None] * inv[None, :]
    cos, sin = jnp.cos(ang), jnp.sin(ang)
    cos128 = jnp.tile(jnp.concatenate([cos, cos], axis=1), (reps, LANES // QK_ROPE))
    sin128 = jnp.tile(jnp.concatenate([-sin, sin], axis=1), (reps, LANES // QK_ROPE))
    return cos128, sin128


def _prep_weights(g_mix, w_in, g_q, w_uq, g_kv, w_uk, w_uv, g_mla_out, g_hg_out, w_o, g_ffn,
                  w_ffn_gate, w_ffn_up, w_ffn_down, g_ple, w_ple_gate, w_ple_proj, g_final):
    row = lambda g: g.reshape(1, -1).astype(F32)
    half = QK_ROPE // 2
    lo = Q_LORA + KV_LORA
    w_main = jnp.concatenate([w_in[:, :lo], w_in[:, lo + QK_ROPE:]], axis=1).astype(BF16)
    w_k = w_in[:, lo:lo + QK_ROPE]
    lane_pad = lambda a: jnp.pad(a, [(0, 0)] * (a.ndim - 1) + [(0, LANES - a.shape[-1])])
    w_kpe = jnp.concatenate([lane_pad(w_k), lane_pad(jnp.roll(w_k, half, axis=1))], axis=1).astype(BF16)
    w_nope = w_uq[:, :, :QK_NOPE].reshape(Q_LORA, MLA_HEADS * QK_NOPE).astype(BF16)
    pe = w_uq[:, :, QK_NOPE:]
    w_pe = lane_pad(pe).reshape(Q_LORA, MLA_HEADS * LANES).astype(BF16)
    w_pe_rot = lane_pad(jnp.roll(pe, half, axis=2)).reshape(Q_LORA, MLA_HEADS * LANES).astype(BF16)
    w_ukT = jnp.transpose(w_uk, (1, 2, 0)).astype(BF16)
    w_uvT = jnp.transpose(w_uv, (1, 0, 2)).astype(BF16)
    return dict(
        g_mix=row(g_mix), w_main=w_main, w_kpe=w_kpe, g_q=row(g_q), g_kv=row(g_kv),
        w_nope=w_nope, w_pe=w_pe, w_pe_rot=w_pe_rot, w_ukT=w_ukT, w_uvT=w_uvT,
        g_mla=row(g_mla_out), g_hg=row(g_hg_out), w_o=w_o.astype(BF16), g_ffn=row(g_ffn),
        w_gate=w_ffn_gate.astype(BF16), w_up=w_ffn_up.astype(BF16), w_down=w_ffn_down.astype(BF16),
        g_ple=row(g_ple), w_ple_gate=w_ple_gate.astype(BF16), w_ple_proj=w_ple_proj.astype(BF16),
        g_final=row(g_final))


def _front(x, pos, w):
    b, t, d = x.shape
    cos128, sin128 = _rope_tables(pos, b)
    x2 = x.reshape(b * t, d)
    z, kpe, kpe_b = _in_proj(x2, w["g_mix"], cos128, sin128, w["w_main"], w["w_kpe"])
    q_cat, k_cat, ckv = _mla_q(z, kpe_b, cos128, sin128, w["g_q"], w["g_kv"],
                               w["w_nope"], w["w_pe"], w["w_pe_rot"], w["w_ukT"])
    return x2, z, q_cat, k_cat, ckv, kpe


def _back(x2, o_lat, o_hg, p, w):
    n = x2.shape[0]
    h = _out_proj(x2, o_lat, o_hg, w["w_uvT"], w["g_mla"], w["w_o"])
    h = _ffn(h, w["g_ffn"], w["w_gate"], w["w_up"], w["w_down"])
    return _ple_final(h, p.reshape(n, -1), w["g_ple"], w["w_ple_gate"], w["w_ple_proj"], w["g_final"])


def kernel(x_prompt, x_sample, cache_kv_latent, cache_k_rope, state_hgrn, page_table, p_prompt, p_sample, g_mix, w_in, g_q, w_uq, g_kv, w_uk, w_uv, g_mla_out, hg_lb, g_hg_out, w_o, g_ffn, w_ffn_gate, w_ffn_up, w_ffn_down, g_ple, w_ple_gate, w_ple_proj, g_final):
    depth = g_mix.shape[0]
    assert depth == 1 and hg_lb.shape[0] == 2, "single-layer trunk only"
    w = _prep_weights(g_mix[0], w_in[0], g_q[0], w_uq[0], g_kv[0], w_uk[0], w_uv[0], g_mla_out[0],
                      g_hg_out[0], w_o[0], g_ffn[0], w_ffn_gate[0], w_ffn_up[0], w_ffn_down[0],
                      g_ple[0], w_ple_gate[0], w_ple_proj[0], g_final)
    hg_lb = hg_lb.astype(F32)
    bp, tp, d = x_prompt.shape
    bs, ts, _ = x_sample.shape
    past_len = page_table.shape[1] * PAGE_SIZE

    xp2, zp, q_cat_p, k_cat_p, ckv_p, kpe_p = _front(x_prompt, jnp.arange(tp, dtype=jnp.int32), w)
    pos_s = past_len + jnp.arange(ts, dtype=jnp.int32)
    xs2, zs, q_cat_s, _, ckv_s, kpe_s = _front(x_sample, pos_s, w)
    n_pool = cache_kv_latent.shape[1]
    o_lat_s, o_lat_p = _attention(
        page_table, q_cat_s.reshape(bs, ts * MLA_HEADS, QK_CAT),
        ckv_s.reshape(bs, ts, KV_LORA), kpe_s.reshape(bs, ts, QK_ROPE),
        cache_kv_latent.reshape(n_pool, PAGE_SIZE, KV_LORA),
        jnp.swapaxes(cache_k_rope.reshape(n_pool, PAGE_SIZE, QK_ROPE), 1, 2),
        q_cat_p, k_cat_p, bp, tp)

    o_hg, st_p = _hgrn_prompt(zp, hg_lb, w["g_hg"], bp, tp)
    y_prompt = _back(xp2, o_lat_p, o_hg, p_prompt, w).reshape(bp, tp, d)
    lat_p = ckv_p.reshape(1, bp, tp, KV_LORA)
    rope_p = kpe_p.reshape(1, bp, tp, QK_ROPE)

    o_hg, st_s = _hgrn_sample(zs, state_hgrn.reshape(state_hgrn.shape[1:]), hg_lb, w["g_hg"], ts)
    y_sample = _back(xs2, o_lat_s.reshape(bs * ts, MLA_HEADS * KV_LORA), o_hg, p_sample, w).reshape(bs, ts, d)
    lat_s = ckv_s.reshape(1, bs, ts, KV_LORA)
    rope_s = kpe_s.reshape(1, bs, ts, QK_ROPE)

    return (y_prompt, y_sample, lat_p, rope_p, st_p[None], lat_s, rope_s, st_s[None])
```

```python
import functools
import math

import jax
import jax.numpy as jnp
from jax import lax
from jax.experimental import pallas as pl
from jax.experimental.pallas import tpu as pltpu

F32 = jnp.float32
BF16 = jnp.bfloat16

EPS = 1e-6
ROPE_BASE = 10000.0
PAGE_SIZE = 128
MLA_HEADS = 8
V_HEAD = 128
QK_NOPE = 128
QK_ROPE = 64
Q_LORA = 512
KV_LORA = 512
ATTN_SCALE = (QK_NOPE + QK_ROPE) ** -0.5
HG_HEADS = 8
HG_DK = 128
HG_DV = 128
HG_CHUNK = 64
HG_SUB = 16
LOG2E = math.log2(math.e)

V7X_VMEM_BYTES = 64 * 1024 * 1024
LANES = 128
SUBLANES = 8

QK_CAT = KV_LORA + LANES

_NT = (((1,), (1,)), ((), ()))
_TN = (((0,), (0,)), ((), ()))


def _vmem_limit(nbytes):
    return int(min(max(nbytes, 16 * 1024 * 1024), V7X_VMEM_BYTES - 8 * 1024 * 1024))


def _nbytes(shape, dtype):
    return math.prod(shape) * jnp.dtype(dtype).itemsize


def _rms(x, g):
    ms = jnp.mean(x * x, axis=-1, keepdims=True)
    return x * lax.rsqrt(ms + EPS) * g


def _dot(a, b):
    return jnp.dot(a, b, preferred_element_type=F32)


def _sigmoid(x):
    return 1.0 / (1.0 + jnp.exp(-x))


def _in_proj_kernel(x_ref, g_ref, cos_ref, sin_ref, wm_ref, wk_ref,
                    z_ref, kpe_ref, kpeb_ref, u_scr):
    @pl.when(pl.program_id(1) == 0)
    def _():
        u = _rms(x_ref[...], g_ref[...]).astype(BF16)
        u_scr[...] = u
        kk = _dot(u, wk_ref[...])
        kpe = kk[:, :LANES] * cos_ref[...] + kk[:, LANES:] * sin_ref[...]
        kpe_ref[...] = kpe[:, :QK_ROPE]
        kpeb_ref[...] = kpe.astype(BF16)

    z_ref[...] = _dot(u_scr[...], wm_ref[...])


def _in_proj(x, g, cos128, sin128, w_main, w_kpe):
    n, d = x.shape
    nout = w_main.shape[1]
    tm = min(1024, n)
    tn = 1024
    assert n % tm == 0 and nout % tn == 0
    est = (2 * (_nbytes((tm, d), F32) + _nbytes((d, tn), BF16) + _nbytes((tm, tn), F32))
           + 2 * _nbytes((tm, d), BF16) + _nbytes((tm, d), F32))
    return pl.pallas_call(
        _in_proj_kernel,
        grid=(n // tm, nout // tn),
        in_specs=[
            pl.BlockSpec((tm, d), lambda i, j: (i, 0)),
            pl.BlockSpec((1, d), lambda i, j: (0, 0)),
            pl.BlockSpec((tm, LANES), lambda i, j: (i, 0)),
            pl.BlockSpec((tm, LANES), lambda i, j: (i, 0)),
            pl.BlockSpec((d, tn), lambda i, j: (0, j)),
            pl.BlockSpec((d, 2 * LANES), lambda i, j: (0, 0)),
        ],
        out_specs=[
            pl.BlockSpec((tm, tn), lambda i, j: (i, j)),
            pl.BlockSpec((tm, QK_ROPE), lambda i, j: (i, 0)),
            pl.BlockSpec((tm, LANES), lambda i, j: (i, 0)),
        ],
        out_shape=[
            jax.ShapeDtypeStruct((n, nout), F32),
            jax.ShapeDtypeStruct((n, QK_ROPE), F32),
            jax.ShapeDtypeStruct((n, LANES), BF16),
        ],
        scratch_shapes=[pltpu.VMEM((tm, d), BF16)],
        compiler_params=pltpu.CompilerParams(
            dimension_semantics=("parallel", "arbitrary"),
            vmem_limit_bytes=_vmem_limit(est)),
        name="in_proj",
    )(x, g, cos128, sin128, w_main, w_kpe)


def _mla_q_kernel(cq_ref, ckv_ref, kpeb_ref, cos_ref, sin_ref, gq_ref, gkv_ref,
                  wn_ref, wp_ref, wpr_ref, wuk_ref,
                  qcat_ref, kcat_ref, ckv_out_ref):
    cqn = _rms(cq_ref[...], gq_ref[...]).astype(BF16)
    qn = _dot(cqn, wn_ref[...]).astype(BF16)
    x = _dot(cqn, wp_ref[...])
    xr = _dot(cqn, wpr_ref[...])
    c, s = cos_ref[...], sin_ref[...]
    for h in range(MLA_HEADS):
        col = h * QK_CAT
        qa = _dot(qn[:, h * QK_NOPE:(h + 1) * QK_NOPE], wuk_ref[h])
        qcat_ref[:, col:col + KV_LORA] = (qa * ATTN_SCALE).astype(BF16)
        lanes = slice(h * LANES, (h + 1) * LANES)
        qcat_ref[:, col + KV_LORA:col + QK_CAT] = (
            (x[:, lanes] * c + xr[:, lanes] * s) * ATTN_SCALE).astype(BF16)
    ckv = _rms(ckv_ref[...], gkv_ref[...])
    ckv_out_ref[...] = ckv
    kcat_ref[:, :KV_LORA] = ckv.astype(BF16)
    kcat_ref[:, KV_LORA:] = kpeb_ref[...]


def _mla_q(z, kpe_b, cos128, sin128, g_q, g_kv, w_nope, w_pe, w_pe_rot, w_ukT):
    n = z.shape[0]
    tm = min(512, n)
    assert n % tm == 0
    qw = MLA_HEADS * QK_CAT
    est = (2 * (2 * _nbytes((tm, Q_LORA), F32) + 2 * _nbytes((tm, LANES), F32) + _nbytes((tm, LANES), BF16)
                + _nbytes((tm, qw), BF16) + _nbytes((tm, QK_CAT), BF16) + _nbytes((tm, KV_LORA), F32)
                + 3 * _nbytes(w_nope.shape, BF16) + _nbytes(w_ukT.shape, BF16))
           + 6 * _nbytes((tm, MLA_HEADS * QK_NOPE), F32))
    full = lambda shape: pl.BlockSpec(shape, lambda i: (0,) * len(shape))
    return pl.pallas_call(
        _mla_q_kernel,
        grid=(n // tm,),
        in_specs=[
            pl.BlockSpec((tm, Q_LORA), lambda i: (i, 0)),
            pl.BlockSpec((tm, KV_LORA), lambda i: (i, 1)),
            pl.BlockSpec((tm, LANES), lambda i: (i, 0)),
            pl.BlockSpec((tm, LANES), lambda i: (i, 0)),
            pl.BlockSpec((tm, LANES), lambda i: (i, 0)),
            full((1, Q_LORA)), full((1, KV_LORA)),
            full(w_nope.shape), full(w_pe.shape), full(w_pe_rot.shape), full(w_ukT.shape),
        ],
        out_specs=[
            pl.BlockSpec((tm, qw), lambda i: (i, 0)),
            pl.BlockSpec((tm, QK_CAT), lambda i: (i, 0)),
            pl.BlockSpec((tm, KV_LORA), lambda i: (i, 0)),
        ],
        out_shape=[
            jax.ShapeDtypeStruct((n, qw), BF16),
            jax.ShapeDtypeStruct((n, QK_CAT), BF16),
            jax.ShapeDtypeStruct((n, KV_LORA), F32),
        ],
        compiler_params=pltpu.CompilerParams(
            dimension_semantics=("parallel",), vmem_limit_bytes=_vmem_limit(est)),
        name="mla_q",
    )(z, z, kpe_b, cos128, sin128, g_q, g_kv, w_nope, w_pe, w_pe_rot, w_ukT)


def _lane_tile(x, width):
    reps = width // LANES
    return x if reps == 1 else jnp.concatenate([x] * reps, axis=1)


def _softmax_update(s, v, m_scr, l_scr, acc_scr):
    m_prev = m_scr[...]
    m_new = jnp.maximum(m_prev, jnp.max(s, axis=1, keepdims=True))
    alpha = jnp.exp(m_prev - m_new)
    p = jnp.exp(s - _lane_tile(m_new, s.shape[1]))
    l_scr[...] = alpha * l_scr[...] + jnp.sum(p, axis=1, keepdims=True)
    m_scr[...] = m_new
    acc_scr[...] = (acc_scr[...] * _lane_tile(alpha, acc_scr.shape[1])
                    + _dot(p.astype(BF16), v))


def _softmax_init(m_scr, l_scr, acc_scr):
    m_scr[...] = jnp.full(m_scr.shape, -jnp.inf, F32)
    l_scr[...] = jnp.zeros(l_scr.shape, F32)
    acc_scr[...] = jnp.zeros(acc_scr.shape, F32)


def _softmax_result(l_scr, acc_scr):
    return acc_scr[...] / _lane_tile(l_scr[...], acc_scr.shape[1])


def _attn_prompt_kernel(q_ref, k_ref, o_ref, qs_scr, m_scr, l_scr, acc_scr, *, tq, tk):
    i = pl.program_id(1)
    rows = tq * MLA_HEADS
    for h in range(MLA_HEADS):
        qs_scr[h * tq:(h + 1) * tq, :] = q_ref[:, h * QK_CAT:(h + 1) * QK_CAT]
    _softmax_init(m_scr, l_scr, acc_scr)

    def scores(start):
        return lax.dot_general(qs_scr[...], k_ref[pl.ds(start, tk), :], _NT,
                               preferred_element_type=F32)

    def full_chunk(c, carry):
        start = pl.multiple_of(c * tk, tk)
        _softmax_update(scores(start), k_ref[pl.ds(start, tk), :KV_LORA], m_scr, l_scr, acc_scr)
        return carry

    n_full = (i * tq) // tk
    lax.fori_loop(0, n_full, full_chunk, 0)

    start = pl.multiple_of(n_full * tk, tk)
    row_tok = i * tq + jnp.bitwise_and(lax.broadcasted_iota(jnp.int32, (rows, tk), 0), tq - 1)
    col = start + lax.broadcasted_iota(jnp.int32, (rows, tk), 1)
    s = jnp.where(col <= row_tok, scores(start), -jnp.inf)
    _softmax_update(s, k_ref[pl.ds(start, tk), :KV_LORA], m_scr, l_scr, acc_scr)
    res = _softmax_result(l_scr, acc_scr).astype(o_ref.dtype)
    for h in range(MLA_HEADS):
        o_ref[:, h * KV_LORA:(h + 1) * KV_LORA] = res[h * tq:(h + 1) * tq, :]


def _attn_prompt(q_cat, k_cat, b, t):
    tq = min(128, t)
    tk = min(512, t)
    assert t % tk == 0 and tk % tq == 0 and tq & (tq - 1) == 0
    rows = tq * MLA_HEADS
    qw, ow = MLA_HEADS * QK_CAT, MLA_HEADS * KV_LORA
    est = (2 * (_nbytes((tq, qw), BF16) + _nbytes((t, QK_CAT), BF16) + _nbytes((tq, ow), BF16))
           + _nbytes((rows, QK_CAT), BF16) + 2 * _nbytes((rows, LANES), F32)
           + _nbytes((rows, KV_LORA), F32) + 6 * _nbytes((rows, tk), F32))
    nq = t // tq
    return pl.pallas_call(
        functools.partial(_attn_prompt_kernel, tq=tq, tk=tk),
        grid=(b, nq),
        in_specs=[
            pl.BlockSpec((tq, qw), lambda bi, i: (bi * nq + i, 0)),
            pl.BlockSpec((t, QK_CAT), lambda bi, i: (bi, 0)),
        ],
        out_specs=pl.BlockSpec((tq, ow), lambda bi, i: (bi * nq + i, 0)),
        out_shape=jax.ShapeDtypeStruct((b * t, ow), BF16),
        scratch_shapes=[pltpu.VMEM((rows, QK_CAT), BF16),
                        pltpu.VMEM((rows, LANES), F32), pltpu.VMEM((rows, LANES), F32),
                        pltpu.VMEM((rows, KV_LORA), F32)],
        compiler_params=pltpu.CompilerParams(
            dimension_semantics=("parallel", "parallel"), vmem_limit_bytes=_vmem_limit(est)),
        name="attn_prompt",
    )(q_cat, k_cat)


def _attn_sample_kernel(pt_ref, q_ref, cn_ref, kn_ref, lat_hbm, rope_hbm, o_ref,
                        lat_buf, rope_buf, sems, latb_scr, ropeb_scr, m_scr, l_scr, acc_scr,
                        *, pages, group, n_chunks):
    b = pl.program_id(0)
    rows, t_new = q_ref.shape[0], cn_ref.shape[0]

    streams = ((lat_hbm, lat_buf), (rope_hbm, rope_buf))

    def page_copy(stream, page, slot, k):
        hbm, buf = streams[stream]
        return pltpu.make_async_copy(hbm.at[page], buf.at[slot, k], sems.at[stream, slot])

    def start_chunk(g, slot):
        for k in range(pages):
            page = pt_ref[g * pages + k]
            for stream in range(2):
                page_copy(stream, page, slot, k).start()

    def wait_chunk(slot):
        for stream in range(2):
            for k in range(pages):
                page_copy(stream, 0, slot, k).wait()

    @pl.when(b == 0)
    def _():
        start_chunk(0, 0)

    qa = q_ref[:, :KV_LORA]
    qp = q_ref[:, KV_LORA:KV_LORA + QK_ROPE]
    _softmax_init(m_scr, l_scr, acc_scr)

    pending = None
    groups_per_chunk = pages // group
    for gi in range(n_chunks * groups_per_chunk):
        c, j = divmod(gi, groups_per_chunk)
        slot = c % 2
        if j == 0:
            start_chunk(b * n_chunks + c + 1, 1 - slot)
            wait_chunk(slot)
        if gi == 0:
            pad = LANES - t_new
            cn = jnp.concatenate([cn_ref[...], jnp.zeros((pad, KV_LORA), F32)], axis=0).astype(BF16)
            kn = jnp.concatenate([kn_ref[...], jnp.zeros((pad, QK_ROPE), F32)], axis=0).astype(BF16)
            sn = lax.dot_general(qa, cn, _NT, preferred_element_type=F32)
            sn = sn + lax.dot_general(qp, kn, _NT, preferred_element_type=F32)
            row_tok = lax.shift_right_logical(lax.broadcasted_iota(jnp.int32, (rows, LANES), 0), 3)
            col = lax.broadcasted_iota(jnp.int32, (rows, LANES), 1)
            pending = (jnp.where(col <= row_tok, sn, -jnp.inf), lambda cn=cn: cn)
        for k in range(j * group, (j + 1) * group):
            latb_scr[k * PAGE_SIZE:(k + 1) * PAGE_SIZE, :] = lat_buf[slot, k].astype(BF16)
            ropeb_scr[:, k * PAGE_SIZE:(k + 1) * PAGE_SIZE] = rope_buf[slot, k].astype(BF16)
        keys = slice(j * group * PAGE_SIZE, (j + 1) * group * PAGE_SIZE)
        s = lax.dot_general(qa, latb_scr[keys, :], _NT, preferred_element_type=F32)
        s = s + _dot(qp, ropeb_scr[:, keys])
        _softmax_update(pending[0], pending[1](), m_scr, l_scr, acc_scr)
        pending = (s, lambda keys=keys: latb_scr[keys, :])
    _softmax_update(pending[0], pending[1](), m_scr, l_scr, acc_scr)
    o_ref[...] = _softmax_result(l_scr, acc_scr).astype(o_ref.dtype)

    @pl.when(b == pl.num_programs(0) - 1)
    def _():
        wait_chunk(n_chunks % 2)


def _attn_sample(page_table, q_cat, ckv_new, kpe_new, cache_lat, cache_rope_t):
    s_, rows, _ = q_cat.shape
    t_new = ckv_new.shape[1]
    n_pages = page_table.shape[1]
    pages = math.gcd(n_pages, 32)
    group = math.gcd(pages, 16)
    n_chunks = n_pages // pages
    assert MLA_HEADS == 8 and rows == t_new * MLA_HEADS and t_new <= LANES
    assert n_chunks % 2 == 0, "buffer slot parity must carry across sequences"
    assert pages // group >= 2, "a group's bf16 keys must outlive the next group's conversion"
    pt_flat = page_table.reshape(-1)
    pt_flat = jnp.concatenate([pt_flat, pt_flat[:pages]])
    keys = pages * PAGE_SIZE
    est = (2 * pages * (_nbytes((PAGE_SIZE, KV_LORA), F32) + _nbytes((QK_ROPE, PAGE_SIZE), F32))
           + _nbytes((keys, KV_LORA), BF16) + _nbytes((QK_ROPE, keys), BF16)
           + 8 * _nbytes((rows, group * PAGE_SIZE), F32) + 8 * _nbytes((rows, KV_LORA), F32)
           + 4 * _nbytes((rows, QK_CAT), BF16))
    grid_spec = pltpu.PrefetchScalarGridSpec(
        num_scalar_prefetch=1,
        grid=(s_,),
        in_specs=[
            pl.BlockSpec((None, rows, QK_CAT), lambda b, pt: (b, 0, 0)),
            pl.BlockSpec((None, t_new, KV_LORA), lambda b, pt: (b, 0, 0)),
            pl.BlockSpec((None, t_new, QK_ROPE), lambda b, pt: (b, 0, 0)),
            pl.BlockSpec(memory_space=pl.ANY),
            pl.BlockSpec(memory_space=pl.ANY),
        ],
        out_specs=pl.BlockSpec((None, rows, KV_LORA), lambda b, pt: (b, 0, 0)),
        scratch_shapes=[
            pltpu.VMEM((2, pages, PAGE_SIZE, KV_LORA), F32),
            pltpu.VMEM((2, pages, QK_ROPE, PAGE_SIZE), F32),
            pltpu.SemaphoreType.DMA((2, 2)),
            pltpu.VMEM((keys, KV_LORA), BF16), pltpu.VMEM((QK_ROPE, keys), BF16),
            pltpu.VMEM((rows, LANES), F32), pltpu.VMEM((rows, LANES), F32),
            pltpu.VMEM((rows, KV_LORA), F32)],
    )
    return pl.pallas_call(
        functools.partial(_attn_sample_kernel, pages=pages, group=group, n_chunks=n_chunks),
        grid_spec=grid_spec,
        out_shape=jax.ShapeDtypeStruct((s_, rows, KV_LORA), BF16),
        compiler_params=pltpu.CompilerParams(
            dimension_semantics=("arbitrary",), vmem_limit_bytes=_vmem_limit(est)),
        name="attn_sample",
    )(pt_flat, q_cat, ckv_new, kpe_new, cache_lat, cache_rope_t)


def _hgrn_chunk(hq, hf, v, hg, sts, lb, g_out, tri):
    nh = len(sts)
    c = hq.shape[0]
    sub = min(HG_SUB, c)
    n_sub = c // sub
    n_rg = sub // SUBLANES
    hs = [slice(h * HG_DK, (h + 1) * HG_DK) for h in range(nh)]
    q = hq * _sigmoid(hq)
    log_f = jnp.log(lb + (1.0 - lb) * _sigmoid(hf))
    k = (1.0 - lb) * _sigmoid(-hf)
    b = jnp.dot(tri, log_f, preferred_element_type=F32, precision=lax.Precision.HIGHEST)
    b2 = b * LOG2E
    vb = v.astype(BF16)
    row = lax.broadcasted_iota(jnp.int32, (SUBLANES, hq.shape[1]), 0)
    o_sub = []
    for i in range(n_sub):
        lo = i * sub
        acc = [[jnp.zeros((SUBLANES, HG_DV), F32) for _ in range(n_rg)] for _ in range(nh)]
        for s in range(sub):
            ks, bs, vs = k[lo + s:lo + s + 1], b2[lo + s:lo + s + 1], v[lo + s:lo + s + 1]
            for rg in range(s // SUBLANES, n_rg):
                r0 = lo + rg * SUBLANES
                e = jnp.exp2(b2[r0:r0 + SUBLANES] - bs)
                if rg == s // SUBLANES:
                    e = jnp.where(row >= s % SUBLANES, e, 0.0)
                prod = q[r0:r0 + SUBLANES] * ks * e
                for h in range(nh):
                    a_col = jnp.sum(prod[:, hs[h]], axis=1, keepdims=True)
                    acc[h][rg] = acc[h][rg] + a_col * vs[:, hs[h]]
        o_i = [a[0] if n_rg == 1 else jnp.concatenate(a, axis=0) for a in acc]
        if i > 0:
            r = b[lo - 1:lo]
            qh = (q[lo:lo + sub] * jnp.exp(b[lo:lo + sub] - r)).astype(BF16)
            kh = (k[:lo] * jnp.exp(r - b[:lo])).astype(BF16)
            for h in range(nh):
                a_off = lax.dot_general(qh[:, hs[h]], kh[:, hs[h]], _NT,
                                        preferred_element_type=F32)
                o_i[h] = o_i[h] + _dot(a_off.astype(BF16), vb[:lo, hs[h]])
        o_sub.append(o_i)
    qe = (q * jnp.exp(b)).astype(BF16)
    b_last = b[c - 1:c]
    kd = (k * jnp.exp(b_last - b)).astype(BF16)
    decay = jnp.exp(b_last)
    gate = hg * _sigmoid(hg)
    outs, new_sts = [], []
    for h in range(nh):
        o = o_sub[0][h] if n_sub == 1 else jnp.concatenate([p[h] for p in o_sub], axis=0)
        o = o + lax.dot_general(qe[:, hs[h]], sts[h].astype(BF16), _NT, preferred_element_type=F32)
        new_sts.append(sts[h] * decay[:, hs[h]]
                       + lax.dot_general(vb[:, hs[h]], kd[:, hs[h]], _TN, preferred_element_type=F32))
        outs.append(_rms(o, g_out) * gate[:, hs[h]])
    return outs, new_sts


def _hgrn_lb(lb_ref):
    a = lb_ref[...]
    a0, a1 = a[0:1], a[1:2]
    m = jnp.maximum(a0, a1)
    e0, e1 = jnp.exp(a0 - m), jnp.exp(a1 - m)
    return e0 / (e0 + e1)


def _tri(c):
    return (lax.broadcasted_iota(jnp.int32, (c, c), 0)
            >= lax.broadcasted_iota(jnp.int32, (c, c), 1)).astype(F32)


def _hgrn_prompt_kernel(hq_ref, hf_ref, hi_ref, hg_ref, lb_ref, g_ref, o_ref, s_ref, st_scr,
                        *, chunk, n_chunks, heads):
    step = pl.program_id(2)

    @pl.when(step == 0)
    def _():
        st_scr[...] = jnp.zeros(st_scr.shape, F32)

    lb = _hgrn_lb(lb_ref)
    g = g_ref[...]
    tri = _tri(chunk)

    def body(ci, carry):
        rows = pl.ds(pl.multiple_of(ci * chunk, chunk), chunk)
        outs, new_sts = _hgrn_chunk(hq_ref[rows, :], hf_ref[rows, :], hi_ref[rows, :], hg_ref[rows, :],
                                    [st_scr[h] for h in range(heads)], lb, g, tri)
        o_ref[rows, :] = jnp.concatenate(outs, axis=1).astype(o_ref.dtype)
        for h in range(heads):
            st_scr[h] = new_sts[h]
        return carry

    lax.fori_loop(0, n_chunks, body, 0)

    @pl.when(step == pl.num_programs(2) - 1)
    def _():
        for h in range(heads):
            s_ref[h] = st_scr[h].T


def _hgrn_prompt(z, hg_lb, g_out, b, t):
    chunk = math.gcd(t, HG_CHUNK)
    n_chunks = min(8, t // chunk)
    tb = chunk * n_chunks
    heads = 8
    assert t % tb == 0 and HG_HEADS % heads == 0
    steps = t // tb
    w = heads * HG_DK
    col0 = (Q_LORA + KV_LORA) // w
    groups = HG_HEADS // heads

    def zspec(group):
        return pl.BlockSpec((tb, w), lambda bi, h, s: (bi * steps + s, col0 + group * groups + h))

    est = (2 * (4 * _nbytes((tb, w), F32) + _nbytes((tb, w), BF16))
           + (3 + 64) * heads * _nbytes((HG_DK, HG_DV), F32))
    return pl.pallas_call(
        functools.partial(_hgrn_prompt_kernel, chunk=chunk, n_chunks=n_chunks, heads=heads),
        grid=(b, groups, steps),
        in_specs=[zspec(0), zspec(1), zspec(2), zspec(3),
                  pl.BlockSpec((2, w), lambda bi, h, s: (0, h)),
                  pl.BlockSpec((1, HG_DV), lambda bi, h, s: (0, 0))],
        out_specs=[pl.BlockSpec((tb, w), lambda bi, h, s: (bi * steps + s, h)),
                   pl.BlockSpec((None, heads, HG_DK, HG_DV), lambda bi, h, s: (bi, h, 0, 0))],
        out_shape=[jax.ShapeDtypeStruct((b * t, HG_HEADS * HG_DV), BF16),
                   jax.ShapeDtypeStruct((b, HG_HEADS, HG_DK, HG_DV), F32)],
        scratch_shapes=[pltpu.VMEM((heads, HG_DV, HG_DK), F32)],
        compiler_params=pltpu.CompilerParams(
            dimension_semantics=("parallel", "parallel", "arbitrary"),
            vmem_limit_bytes=_vmem_limit(est)),
        name="hgrn_prompt",
    )(z, z, z, z, hg_lb, g_out)


def _hgrn_sample_kernel(hq_ref, hf_ref, hi_ref, hg_ref, s0_ref, lb_ref, g_ref, o_ref, s_ref):
    outs, new_sts = _hgrn_chunk(hq_ref[...], hf_ref[...], hi_ref[...], hg_ref[...],
                                [s0_ref[h].T for h in range(HG_HEADS)],
                                _hgrn_lb(lb_ref), g_ref[...], _tri(hq_ref.shape[0]))
    o_ref[...] = jnp.concatenate(outs, axis=1)
    for h in range(HG_HEADS):
        s_ref[h] = new_sts[h].T


def _hgrn_sample(z, state, hg_lb, g_out, t):
    nseq = state.shape[0]
    assert HG_CHUNK % t == 0 and t % SUBLANES == 0
    w = HG_HEADS * HG_DK
    col0 = (Q_LORA + KV_LORA) // w
    zspec = lambda group: pl.BlockSpec((t, w), lambda bi: (bi, col0 + group))
    sspec = pl.BlockSpec((None, HG_HEADS, HG_DK, HG_DV), lambda bi: (bi, 0, 0, 0))
    est = 2 * (5 * _nbytes((t, w), F32) + 2 * _nbytes((HG_HEADS, HG_DK, HG_DV), F32)) + 64 * _nbytes((HG_DK, HG_DV), F32)
    return pl.pallas_call(
        _hgrn_sample_kernel,
        grid=(nseq,),
        in_specs=[zspec(0), zspec(1), zspec(2), zspec(3), sspec,
                  pl.BlockSpec((2, w), lambda bi: (0, 0)),
                  pl.BlockSpec((1, HG_DV), lambda bi: (0, 0))],
        out_specs=[pl.BlockSpec((t, w), lambda bi: (bi, 0)), sspec],
        out_shape=[jax.ShapeDtypeStruct((nseq * t, w), F32),
                   jax.ShapeDtypeStruct(state.shape, F32)],
        compiler_params=pltpu.CompilerParams(
            dimension_semantics=("parallel",), vmem_limit_bytes=_vmem_limit(est)),
        name="hgrn_sample",
    )(z, z, z, z, state, hg_lb, g_out)


def _out_proj_kernel(x_ref, olat_ref, ohg_ref, wuv_ref, gm_ref, wo_ref, h_ref):
    parts = []
    for h in range(MLA_HEADS):
        om = _dot(olat_ref[:, h * KV_LORA:(h + 1) * KV_LORA], wuv_ref[h])
        parts.append(_rms(om, gm_ref[...]).astype(BF16))
    o_mla = jnp.concatenate(parts, axis=1)
    w = MLA_HEADS * V_HEAD
    h_ref[...] = (x_ref[...] + _dot(o_mla, wo_ref[:w, :])
                  + _dot(ohg_ref[...].astype(BF16), wo_ref[w:, :]))


def _out_proj(x, o_lat, o_hg, w_uvT, g_mla, w_o):
    n, d = x.shape
    tm = min(256, n)
    assert n % tm == 0
    lw = MLA_HEADS * KV_LORA
    hw = o_hg.shape[1]
    est = (2 * (2 * _nbytes((tm, d), F32) + _nbytes((tm, lw), BF16) + _nbytes((tm, hw), o_hg.dtype)
                + _nbytes(w_uvT.shape, BF16) + _nbytes(w_o.shape, BF16))
           + 4 * _nbytes((tm, d), F32))
    full = lambda shape: pl.BlockSpec(shape, lambda i: (0,) * len(shape))
    return pl.pallas_call(
        _out_proj_kernel,
        grid=(n // tm,),
        in_specs=[pl.BlockSpec((tm, d), lambda i: (i, 0)),
                  pl.BlockSpec((tm, lw), lambda i: (i, 0)),
                  pl.BlockSpec((tm, hw), lambda i: (i, 0)),
                  full(w_uvT.shape), full((1, V_HEAD)), full(w_o.shape)],
        out_specs=pl.BlockSpec((tm, d), lambda i: (i, 0)),
        out_shape=jax.ShapeDtypeStruct((n, d), F32),
        compiler_params=pltpu.CompilerParams(
            dimension_semantics=("parallel",), vmem_limit_bytes=_vmem_limit(est)),
        name="out_proj",
    )(x, o_lat, o_hg, w_uvT, g_mla, w_o)


def _ffn_kernel(h_ref, g_ref, wg_ref, wu_ref, wd_ref, o_ref, hn_scr):
    @pl.when(pl.program_id(1) == 0)
    def _():
        h = h_ref[...]
        hn_scr[...] = _rms(h, g_ref[...]).astype(BF16)
        o_ref[...] = h

    hn = hn_scr[...]
    gate = _dot(hn, wg_ref[...])
    up = _dot(hn, wu_ref[...])
    act = (gate * _sigmoid(gate) * up).astype(BF16)
    o_ref[...] += _dot(act, wd_ref[...])


def _ffn(h, g, w_gate, w_up, w_down):
    n, d = h.shape
    dff = w_gate.shape[1]
    tm = min(512, n)
    tf = 512
    assert n % tm == 0 and dff % tf == 0
    est = (2 * (2 * _nbytes((tm, d), F32) + 3 * _nbytes((d, tf), BF16))
           + _nbytes((tm, d), BF16) + 4 * _nbytes((tm, tf), F32) + _nbytes((tm, d), F32))
    return pl.pallas_call(
        _ffn_kernel,
        grid=(n // tm, dff // tf),
        in_specs=[pl.BlockSpec((tm, d), lambda i, j: (i, 0)),
                  pl.BlockSpec((1, d), lambda i, j: (0, 0)),
                  pl.BlockSpec((d, tf), lambda i, j: (0, j)),
                  pl.BlockSpec((d, tf), lambda i, j: (0, j)),
                  pl.BlockSpec((tf, d), lambda i, j: (j, 0))],
        out_specs=pl.BlockSpec((tm, d), lambda i, j: (i, 0)),
        out_shape=jax.ShapeDtypeStruct((n, d), F32),
        scratch_shapes=[pltpu.VMEM((tm, d), BF16)],
        compiler_params=pltpu.CompilerParams(
            dimension_semantics=("parallel", "arbitrary"), vmem_limit_bytes=_vmem_limit(est)),
        name="ffn",
    )(h, g, w_gate, w_up, w_down)


def _ple_final_kernel(h_ref, p_ref, gp_ref, wg_ref, wp_ref, gf_ref, y_ref):
    h = h_ref[...]
    a = _dot(_rms(h, gp_ref[...]).astype(BF16), wg_ref[...])
    pp = _dot(p_ref[...].astype(BF16), wp_ref[...])
    y_ref[...] = _rms(h + pp * _sigmoid(a), gf_ref[...])


def _ple_final(h, p, g_ple, w_gate, w_proj, g_final):
    n, d = h.shape
    pd = p.shape[1]
    tm = min(256, n)
    assert n % tm == 0
    est = (2 * (2 * _nbytes((tm, d), F32) + _nbytes((tm, pd), F32)
                + _nbytes(w_gate.shape, BF16) + _nbytes(w_proj.shape, BF16))
           + 4 * _nbytes((tm, d), F32))
    full = lambda shape: pl.BlockSpec(shape, lambda i: (0,) * len(shape))
    return pl.pallas_call(
        _ple_final_kernel,
        grid=(n // tm,),
        in_specs=[pl.BlockSpec((tm, d), lambda i: (i, 0)),
                  pl.BlockSpec((tm, pd), lambda i: (i, 0)),
                  full((1, d)), full(w_gate.shape), full(w_proj.shape), full((1, d))],
        out_specs=pl.BlockSpec((tm, d), lambda i: (i, 0)),
        out_shape=jax.ShapeDtypeStruct((n, d), F32),
        compiler_params=pltpu.CompilerParams(
            dimension_semantics=("parallel",), vmem_limit_bytes=_vmem_limit(est)),
        name="ple_final",
    )(h, p, g_ple, w_gate, w_proj, g_final)


def _rope_tables(pos, reps):
    inv = ROPE_BASE ** (-jnp.arange(0, QK_ROPE, 2, dtype=F32) / QK_ROPE)
    ang = pos.astype(F32)[:, None] * inv[None, :]
    cos, sin = jnp.cos(ang), jnp.sin(ang)
    cos128 = jnp.tile(jnp.concatenate([cos, cos], axis=1), (reps, LANES // QK_ROPE))
    sin128 = jnp.tile(jnp.concatenate([-sin, sin], axis=1), (reps, LANES // QK_ROPE))
    return cos128, sin128


def _prep_weights(g_mix, w_in, g_q, w_uq, g_kv, w_uk, w_uv, g_mla_out, g_hg_out, w_o, g_ffn,
                  w_ffn_gate, w_ffn_up, w_ffn_down, g_ple, w_ple_gate, w_ple_proj, g_final):
    row = lambda g: g.reshape(1, -1).astype(F32)
    half = QK_ROPE // 2
    lo = Q_LORA + KV_LORA
    w_main = jnp.concatenate([w_in[:, :lo], w_in[:, lo + QK_ROPE:]], axis=1).astype(BF16)
    w_k = w_in[:, lo:lo + QK_ROPE]
    lane_pad = lambda a: jnp.pad(a, [(0, 0)] * (a.ndim - 1) + [(0, LANES - a.shape[-1])])
    w_kpe = jnp.concatenate([lane_pad(w_k), lane_pad(jnp.roll(w_k, half, axis=1))], axis=1).astype(BF16)
    w_nope = w_uq[:, :, :QK_NOPE].reshape(Q_LORA, MLA_HEADS * QK_NOPE).astype(BF16)
    pe = w_uq[:, :, QK_NOPE:]
    w_pe = lane_pad(pe).reshape(Q_LORA, MLA_HEADS * LANES).astype(BF16)
    w_pe_rot = lane_pad(jnp.roll(pe, half, axis=2)).reshape(Q_LORA, MLA_HEADS * LANES).astype(BF16)
    w_ukT = jnp.transpose(w_uk, (1, 2, 0)).astype(BF16)
    w_uvT = jnp.transpose(w_uv, (1, 0, 2)).astype(BF16)
    return dict(
        g_mix=row(g_mix), w_main=w_main, w_kpe=w_kpe, g_q=row(g_q), g_kv=row(g_kv),
        w_nope=w_nope, w_pe=w_pe, w_pe_rot=w_pe_rot, w_ukT=w_ukT, w_uvT=w_uvT,
        g_mla=row(g_mla_out), g_hg=row(g_hg_out), w_o=w_o.astype(BF16), g_ffn=row(g_ffn),
        w_gate=w_ffn_gate.astype(BF16), w_up=w_ffn_up.astype(BF16), w_down=w_ffn_down.astype(BF16),
        g_ple=row(g_ple), w_ple_gate=w_ple_gate.astype(BF16), w_ple_proj=w_ple_proj.astype(BF16),
        g_final=row(g_final))


def _front(x, pos, w):
    b, t, d = x.shape
    cos128, sin128 = _rope_tables(pos, b)
    x2 = x.reshape(b * t, d)
    z, kpe, kpe_b = _in_proj(x2, w["g_mix"], cos128, sin128, w["w_main"], w["w_kpe"])
    q_cat, k_cat, ckv = _mla_q(z, kpe_b, cos128, sin128, w["g_q"], w["g_kv"],
                               w["w_nope"], w["w_pe"], w["w_pe_rot"], w["w_ukT"])
    return x2, z, q_cat, k_cat, ckv, kpe


def _back(x2, o_lat, o_hg, p, w):
    n = x2.shape[0]
    h = _out_proj(x2, o_lat, o_hg, w["w_uvT"], w["g_mla"], w["w_o"])
    h = _ffn(h, w["g_ffn"], w["w_gate"], w["w_up"], w["w_down"])
    return _ple_final(h, p.reshape(n, -1), w["g_ple"], w["w_ple_gate"], w["w_ple_proj"], w["g_final"])


def kernel(x_prompt, x_sample, cache_kv_latent, cache_k_rope, state_hgrn, page_table, p_prompt, p_sample, g_mix, w_in, g_q, w_uq, g_kv, w_uk, w_uv, g_mla_out, hg_lb, g_hg_out, w_o, g_ffn, w_ffn_gate, w_ffn_up, w_ffn_down, g_ple, w_ple_gate, w_ple_proj, g_final):
    depth = g_mix.shape[0]
    assert depth == 1 and hg_lb.shape[0] == 2, "single-layer trunk only"
    w = _prep_weights(g_mix[0], w_in[0], g_q[0], w_uq[0], g_kv[0], w_uk[0], w_uv[0], g_mla_out[0],
                      g_hg_out[0], w_o[0], g_ffn[0], w_ffn_gate[0], w_ffn_up[0], w_ffn_down[0],
                      g_ple[0], w_ple_gate[0], w_ple_proj[0], g_final)
    hg_lb = hg_lb.astype(F32)
    bp, tp, d = x_prompt.shape
    bs, ts, _ = x_sample.shape
    past_len = page_table.shape[1] * PAGE_SIZE

    x2, z, q_cat, k_cat, ckv, kpe = _front(x_prompt, jnp.arange(tp, dtype=jnp.int32), w)
    o_lat = _attn_prompt(q_cat, k_cat, bp, tp)
    o_hg, st_p = _hgrn_prompt(z, hg_lb, w["g_hg"], bp, tp)
    y_prompt = _back(x2, o_lat, o_hg, p_prompt, w).reshape(bp, tp, d)
    lat_p = ckv.reshape(1, bp, tp, KV_LORA)
    rope_p = kpe.reshape(1, bp, tp, QK_ROPE)

    pos_s = past_len + jnp.arange(ts, dtype=jnp.int32)
    x2, z, q_cat, k_cat, ckv, kpe = _front(x_sample, pos_s, w)
    n_pool = cache_kv_latent.shape[1]
    o_lat = _attn_sample(page_table, q_cat.reshape(bs, ts * MLA_HEADS, QK_CAT),
                         ckv.reshape(bs, ts, KV_LORA), kpe.reshape(bs, ts, QK_ROPE),
                         cache_kv_latent.reshape(n_pool, PAGE_SIZE, KV_LORA),
                         jnp.swapaxes(cache_k_rope.reshape(n_pool, PAGE_SIZE, QK_ROPE), 1, 2))
    o_hg, st_s = _hgrn_sample(z, state_hgrn.reshape(state_hgrn.shape[1:]), hg_lb, w["g_hg"], ts)
    y_sample = _back(x2, o_lat.reshape(bs * ts, MLA_HEADS * KV_LORA), o_hg, p_sample, w).reshape(bs, ts, d)
    lat_s = ckv.reshape(1, bs, ts, KV_LORA)
    rope_s = kpe.reshape(1, bs, ts, QK_ROPE)

    return (y_prompt, y_sample, lat_p, rope_p, st_p[None], lat_s, rope_s, st_s[None])
```

```python
import functools
import math

import jax
import jax.numpy as jnp
from jax import lax
from jax.experimental import pallas as pl
from jax.experimental.pallas import tpu as pltpu

F32 = jnp.float32
BF16 = jnp.bfloat16

EPS = 1e-6
ROPE_BASE = 10000.0
PAGE_SIZE = 128
MLA_HEADS = 8
V_HEAD = 128
QK_NOPE = 128
QK_ROPE = 64
Q_LORA = 512
KV_LORA = 512
ATTN_SCALE = (QK_NOPE + QK_ROPE) ** -0.5
HG_HEADS = 8
HG_DK = 128
HG_DV = 128
HG_CHUNK = 64
HG_SUB = 16
LOG2E = math.log2(math.e)

V7X_VMEM_BYTES = 64 * 1024 * 1024
LANES = 128
SUBLANES = 8

QK_CAT = KV_LORA + LANES

_NT = (((1,), (1,)), ((), ()))
_TN = (((0,), (0,)), ((), ()))


def _vmem_limit(nbytes):
    return int(min(max(nbytes, 16 * 1024 * 1024), V7X_VMEM_BYTES - 8 * 1024 * 1024))


def _nbytes(shape, dtype):
    return math.prod(shape) * jnp.dtype(dtype).itemsize


def _rms(x, g):
    ms = jnp.mean(x * x, axis=-1, keepdims=True)
    return x * lax.rsqrt(ms + EPS) * g


def _dot(a, b):
    return jnp.dot(a, b, preferred_element_type=F32)


def _sigmoid(x):
    return 1.0 / (1.0 + jnp.exp(-x))


def _in_proj_kernel(x_ref, g_ref, cos_ref, sin_ref, wm_ref, wk_ref,
                    z_ref, kpe_ref, kpeb_ref, u_scr):
    @pl.when(pl.program_id(1) == 0)
    def _():
        u = _rms(x_ref[...], g_ref[...]).astype(BF16)
        u_scr[...] = u
        kk = _dot(u, wk_ref[...])
        kpe = kk[:, :LANES] * cos_ref[...] + kk[:, LANES:] * sin_ref[...]
        kpe_ref[...] = kpe[:, :QK_ROPE]
        kpeb_ref[...] = kpe.astype(BF16)

    z_ref[...] = _dot(u_scr[...], wm_ref[...])


def _in_proj(x, g, cos128, sin128, w_main, w_kpe):
    n, d = x.shape
    nout = w_main.shape[1]
    tm = min(1024, n)
    tn = 1024
    assert n % tm == 0 and nout % tn == 0
    est = (2 * (_nbytes((tm, d), F32) + _nbytes((d, tn), BF16) + _nbytes((tm, tn), F32))
           + 2 * _nbytes((tm, d), BF16) + _nbytes((tm, d), F32))
    return pl.pallas_call(
        _in_proj_kernel,
        grid=(n // tm, nout // tn),
        in_specs=[
            pl.BlockSpec((tm, d), lambda i, j: (i, 0)),
            pl.BlockSpec((1, d), lambda i, j: (0, 0)),
            pl.BlockSpec((tm, LANES), lambda i, j: (i, 0)),
            pl.BlockSpec((tm, LANES), lambda i, j: (i, 0)),
            pl.BlockSpec((d, tn), lambda i, j: (0, j)),
            pl.BlockSpec((d, 2 * LANES), lambda i, j: (0, 0)),
        ],
        out_specs=[
            pl.BlockSpec((tm, tn), lambda i, j: (i, j)),
            pl.BlockSpec((tm, QK_ROPE), lambda i, j: (i, 0)),
            pl.BlockSpec((tm, LANES), lambda i, j: (i, 0)),
        ],
        out_shape=[
            jax.ShapeDtypeStruct((n, nout), F32),
            jax.ShapeDtypeStruct((n, QK_ROPE), F32),
            jax.ShapeDtypeStruct((n, LANES), BF16),
        ],
        scratch_shapes=[pltpu.VMEM((tm, d), BF16)],
        compiler_params=pltpu.CompilerParams(
            dimension_semantics=("parallel", "arbitrary"),
            vmem_limit_bytes=_vmem_limit(est)),
        name="in_proj",
    )(x, g, cos128, sin128, w_main, w_kpe)


def _mla_q_kernel(cq_ref, ckv_ref, kpeb_ref, cos_ref, sin_ref, gq_ref, gkv_ref,
                  wn_ref, wp_ref, wpr_ref, wuk_ref,
                  qcat_ref, kcat_ref, ckv_out_ref):
    cqn = _rms(cq_ref[...], gq_ref[...]).astype(BF16)
    qn = _dot(cqn, wn_ref[...]).astype(BF16)
    x = _dot(cqn, wp_ref[...])
    xr = _dot(cqn, wpr_ref[...])
    c, s = cos_ref[...], sin_ref[...]
    for h in range(MLA_HEADS):
        col = h * QK_CAT
        qa = _dot(qn[:, h * QK_NOPE:(h + 1) * QK_NOPE], wuk_ref[h])
        qcat_ref[:, col:col + KV_LORA] = (qa * ATTN_SCALE).astype(BF16)
        lanes = slice(h * LANES, (h + 1) * LANES)
        qcat_ref[:, col + KV_LORA:col + QK_CAT] = (
            (x[:, lanes] * c + xr[:, lanes] * s) * ATTN_SCALE).astype(BF16)
    ckv = _rms(ckv_ref[...], gkv_ref[...])
    ckv_out_ref[...] = ckv
    kcat_ref[:, :KV_LORA] = ckv.astype(BF16)
    kcat_ref[:, KV_LORA:] = kpeb_ref[...]


def _mla_q(z, kpe_b, cos128, sin128, g_q, g_kv, w_nope, w_pe, w_pe_rot, w_ukT):
    n = z.shape[0]
    tm = min(512, n)
    assert n % tm == 0
    qw = MLA_HEADS * QK_CAT
    est = (2 * (2 * _nbytes((tm, Q_LORA), F32) + 2 * _nbytes((tm, LANES), F32) + _nbytes((tm, LANES), BF16)
                + _nbytes((tm, qw), BF16) + _nbytes((tm, QK_CAT), BF16) + _nbytes((tm, KV_LORA), F32)
                + 3 * _nbytes(w_nope.shape, BF16) + _nbytes(w_ukT.shape, BF16))
           + 6 * _nbytes((tm, MLA_HEADS * QK_NOPE), F32))
    full = lambda shape: pl.BlockSpec(shape, lambda i: (0,) * len(shape))
    return pl.pallas_call(
        _mla_q_kernel,
        grid=(n // tm,),
        in_specs=[
            pl.BlockSpec((tm, Q_LORA), lambda i: (i, 0)),
            pl.BlockSpec((tm, KV_LORA), lambda i: (i, 1)),
            pl.BlockSpec((tm, LANES), lambda i: (i, 0)),
            pl.BlockSpec((tm, LANES), lambda i: (i, 0)),
            pl.BlockSpec((tm, LANES), lambda i: (i, 0)),
            full((1, Q_LORA)), full((1, KV_LORA)),
            full(w_nope.shape), full(w_pe.shape), full(w_pe_rot.shape), full(w_ukT.shape),
        ],
        out_specs=[
            pl.BlockSpec((tm, qw), lambda i: (i, 0)),
            pl.BlockSpec((tm, QK_CAT), lambda i: (i, 0)),
            pl.BlockSpec((tm, KV_LORA), lambda i: (i, 0)),
        ],
        out_shape=[
            jax.ShapeDtypeStruct((n, qw), BF16),
            jax.ShapeDtypeStruct((n, QK_CAT), BF16),
            jax.ShapeDtypeStruct((n, KV_LORA), F32),
        ],
        compiler_params=pltpu.CompilerParams(
            dimension_semantics=("parallel",), vmem_limit_bytes=_vmem_limit(est)),
        name="mla_q",
    )(z, z, kpe_b, cos128, sin128, g_q, g_kv, w_nope, w_pe, w_pe_rot, w_ukT)


def _lane_tile(x, width):
    reps = width // LANES
    return x if reps == 1 else jnp.concatenate([x] * reps, axis=1)


def _softmax_update(s, v, m_scr, l_scr, acc_scr):
    m_prev = m_scr[...]
    m_new = jnp.maximum(m_prev, jnp.max(s, axis=1, keepdims=True))
    alpha = jnp.exp(m_prev - m_new)
    p = jnp.exp(s - _lane_tile(m_new, s.shape[1]))
    l_scr[...] = alpha * l_scr[...] + jnp.sum(p, axis=1, keepdims=True)
    m_scr[...] = m_new
    acc_scr[...] = (acc_scr[...] * _lane_tile(alpha, acc_scr.shape[1])
                    + _dot(p.astype(BF16), v))


def _softmax_init(m_scr, l_scr, acc_scr):
    m_scr[...] = jnp.full(m_scr.shape, -jnp.inf, F32)
    l_scr[...] = jnp.zeros(l_scr.shape, F32)
    acc_scr[...] = jnp.zeros(acc_scr.shape, F32)


def _softmax_result(l_scr, acc_scr):
    return acc_scr[...] / _lane_tile(l_scr[...], acc_scr.shape[1])


def _attn_prompt_kernel(q_ref, k_ref, o_ref, qs_scr, m_scr, l_scr, acc_scr, *, tq, tk):
    i = pl.program_id(1)
    rows = tq * MLA_HEADS
    for h in range(MLA_HEADS):
        qs_scr[h * tq:(h + 1) * tq, :] = q_ref[:, h * QK_CAT:(h + 1) * QK_CAT]
    _softmax_init(m_scr, l_scr, acc_scr)

    def scores(start):
        return lax.dot_general(qs_scr[...], k_ref[pl.ds(start, tk), :], _NT,
                               preferred_element_type=F32)

    def full_chunk(c, carry):
        start = pl.multiple_of(c * tk, tk)
        _softmax_update(scores(start), k_ref[pl.ds(start, tk), :KV_LORA], m_scr, l_scr, acc_scr)
        return carry

    n_full = (i * tq) // tk
    lax.fori_loop(0, n_full, full_chunk, 0)

    start = pl.multiple_of(n_full * tk, tk)
    row_tok = i * tq + jnp.bitwise_and(lax.broadcasted_iota(jnp.int32, (rows, tk), 0), tq - 1)
    col = start + lax.broadcasted_iota(jnp.int32, (rows, tk), 1)
    s = jnp.where(col <= row_tok, scores(start), -jnp.inf)
    _softmax_update(s, k_ref[pl.ds(start, tk), :KV_LORA], m_scr, l_scr, acc_scr)
    res = _softmax_result(l_scr, acc_scr).astype(o_ref.dtype)
    for h in range(MLA_HEADS):
        o_ref[:, h * KV_LORA:(h + 1) * KV_LORA] = res[h * tq:(h + 1) * tq, :]


def _attn_prompt(q_cat, k_cat, b, t):
    tq = min(128, t)
    tk = min(512, t)
    assert t % tk == 0 and tk % tq == 0 and tq & (tq - 1) == 0
    rows = tq * MLA_HEADS
    qw, ow = MLA_HEADS * QK_CAT, MLA_HEADS * KV_LORA
    est = (2 * (_nbytes((tq, qw), BF16) + _nbytes((t, QK_CAT), BF16) + _nbytes((tq, ow), BF16))
           + _nbytes((rows, QK_CAT), BF16) + 2 * _nbytes((rows, LANES), F32)
           + _nbytes((rows, KV_LORA), F32) + 6 * _nbytes((rows, tk), F32))
    nq = t // tq
    return pl.pallas_call(
        functools.partial(_attn_prompt_kernel, tq=tq, tk=tk),
        grid=(b, nq),
        in_specs=[
            pl.BlockSpec((tq, qw), lambda bi, i: (bi * nq + i, 0)),
            pl.BlockSpec((t, QK_CAT), lambda bi, i: (bi, 0)),
        ],
        out_specs=pl.BlockSpec((tq, ow), lambda bi, i: (bi * nq + i, 0)),
        out_shape=jax.ShapeDtypeStruct((b * t, ow), BF16),
        scratch_shapes=[pltpu.VMEM((rows, QK_CAT), BF16),
                        pltpu.VMEM((rows, LANES), F32), pltpu.VMEM((rows, LANES), F32),
                        pltpu.VMEM((rows, KV_LORA), F32)],
        compiler_params=pltpu.CompilerParams(
            dimension_semantics=("parallel", "parallel"), vmem_limit_bytes=_vmem_limit(est)),
        name="attn_prompt",
    )(q_cat, k_cat)


def _attn_sample_kernel(pt_ref, q_ref, cn_ref, kn_ref, lat_hbm, rope_hbm, o_ref,
                        lat_buf, rope_buf, sems, latb_scr, ropeb_scr, m_scr, l_scr, acc_scr,
                        *, pages, group, n_chunks):
    b = pl.program_id(0)
    rows, t_new = q_ref.shape[0], cn_ref.shape[0]

    streams = ((lat_hbm, lat_buf), (rope_hbm, rope_buf))

    def page_copy(stream, page, slot, k):
        hbm, buf = streams[stream]
        return pltpu.make_async_copy(hbm.at[page], buf.at[slot, k], sems.at[stream, slot])

    def start_chunk(g, slot):
        for k in range(pages):
            page = pt_ref[g * pages + k]
            for stream in range(2):
                page_copy(stream, page, slot, k).start()

    def wait_chunk(slot):
        for stream in range(2):
            for k in range(pages):
                page_copy(stream, 0, slot, k).wait()

    @pl.when(b == 0)
    def _():
        start_chunk(0, 0)

    qa = q_ref[:, :KV_LORA]
    qp = q_ref[:, KV_LORA:KV_LORA + QK_ROPE]
    _softmax_init(m_scr, l_scr, acc_scr)

    pending = None
    groups_per_chunk = pages // group
    for gi in range(n_chunks * groups_per_chunk):
        c, j = divmod(gi, groups_per_chunk)
        slot = c % 2
        if j == 0:
            start_chunk(b * n_chunks + c + 1, 1 - slot)
            wait_chunk(slot)
        if gi == 0:
            pad = LANES - t_new
            cn = jnp.concatenate([cn_ref[...], jnp.zeros((pad, KV_LORA), F32)], axis=0).astype(BF16)
            kn = jnp.concatenate([kn_ref[...], jnp.zeros((pad, QK_ROPE), F32)], axis=0).astype(BF16)
            sn = lax.dot_general(qa, cn, _NT, preferred_element_type=F32)
            sn = sn + lax.dot_general(qp, kn, _NT, preferred_element_type=F32)
            row_tok = lax.shift_right_logical(lax.broadcasted_iota(jnp.int32, (rows, LANES), 0), 3)
            col = lax.broadcasted_iota(jnp.int32, (rows, LANES), 1)
            pending = (jnp.where(col <= row_tok, sn, -jnp.inf), lambda cn=cn: cn)
        for k in range(j * group, (j + 1) * group):
            latb_scr[k * PAGE_SIZE:(k + 1) * PAGE_SIZE, :] = lat_buf[slot, k].astype(BF16)
            ropeb_scr[:, k * PAGE_SIZE:(k + 1) * PAGE_SIZE] = rope_buf[slot, k].astype(BF16)
        keys = slice(j * group * PAGE_SIZE, (j + 1) * group * PAGE_SIZE)
        s = lax.dot_general(qa, latb_scr[keys, :], _NT, preferred_element_type=F32)
        s = s + _dot(qp, ropeb_scr[:, keys])
        _softmax_update(pending[0], pending[1](), m_scr, l_scr, acc_scr)
        pending = (s, lambda keys=keys: latb_scr[keys, :])
    _softmax_update(pending[0], pending[1](), m_scr, l_scr, acc_scr)
    o_ref[...] = _softmax_result(l_scr, acc_scr).astype(o_ref.dtype)

    @pl.when(b == pl.num_programs(0) - 1)
    def _():
        wait_chunk(n_chunks % 2)


def _attn_sample(page_table, q_cat, ckv_new, kpe_new, cache_lat, cache_rope_t):
    s_, rows, _ = q_cat.shape
    t_new = ckv_new.shape[1]
    n_pages = page_table.shape[1]
    pages = math.gcd(n_pages, 64)
    group = math.gcd(pages, 16)
    n_chunks = n_pages // pages
    assert MLA_HEADS == 8 and rows == t_new * MLA_HEADS and t_new <= LANES
    assert n_chunks % 2 == 0, "buffer slot parity must carry across sequences"
    assert pages // group >= 2, "a group's bf16 keys must outlive the next group's conversion"
    pt_flat = page_table.reshape(-1)
    pt_flat = jnp.concatenate([pt_flat, pt_flat[:pages]])
    keys = pages * PAGE_SIZE
    est = (2 * pages * (_nbytes((PAGE_SIZE, KV_LORA), F32) + _nbytes((QK_ROPE, PAGE_SIZE), F32))
           + _nbytes((keys, KV_LORA), BF16) + _nbytes((QK_ROPE, keys), BF16)
           + 8 * _nbytes((rows, group * PAGE_SIZE), F32) + 8 * _nbytes((rows, KV_LORA), F32)
           + 4 * _nbytes((rows, QK_CAT), BF16))
    grid_spec = pltpu.PrefetchScalarGridSpec(
        num_scalar_prefetch=1,
        grid=(s_,),
        in_specs=[
            pl.BlockSpec((None, rows, QK_CAT), lambda b, pt: (b, 0, 0)),
            pl.BlockSpec((None, t_new, KV_LORA), lambda b, pt: (b, 0, 0)),
            pl.BlockSpec((None, t_new, QK_ROPE), lambda b, pt: (b, 0, 0)),
            pl.BlockSpec(memory_space=pl.ANY),
            pl.BlockSpec(memory_space=pl.ANY),
        ],
        out_specs=pl.BlockSpec((None, rows, KV_LORA), lambda b, pt: (b, 0, 0)),
        scratch_shapes=[
            pltpu.VMEM((2, pages, PAGE_SIZE, KV_LORA), F32),
            pltpu.VMEM((2, pages, QK_ROPE, PAGE_SIZE), F32),
            pltpu.SemaphoreType.DMA((2, 2)),
            pltpu.VMEM((keys, KV_LORA), BF16), pltpu.VMEM((QK_ROPE, keys), BF16),
            pltpu.VMEM((rows, LANES), F32), pltpu.VMEM((rows, LANES), F32),
            pltpu.VMEM((rows, KV_LORA), F32)],
    )
    return pl.pallas_call(
        functools.partial(_attn_sample_kernel, pages=pages, group=group, n_chunks=n_chunks),
        grid_spec=grid_spec,
        out_shape=jax.ShapeDtypeStruct((s_, rows, KV_LORA), BF16),
        compiler_params=pltpu.CompilerParams(
            dimension_semantics=("arbitrary",), vmem_limit_bytes=_vmem_limit(est)),
        name="attn_sample",
    )(pt_flat, q_cat, ckv_new, kpe_new, cache_lat, cache_rope_t)


def _hgrn_chunk(hq, hf, v, hg, sts, lb, g_out, tri):
    nh = len(sts)
    c = hq.shape[0]
    sub = min(HG_SUB, c)
    n_sub = c // sub
    n_rg = sub // SUBLANES
    hs = [slice(h * HG_DK, (h + 1) * HG_DK) for h in range(nh)]
    q = hq * _sigmoid(hq)
    log_f = jnp.log(lb + (1.0 - lb) * _sigmoid(hf))
    k = (1.0 - lb) * _sigmoid(-hf)
    b = jnp.dot(tri, log_f, preferred_element_type=F32, precision=lax.Precision.HIGHEST)
    b2 = b * LOG2E
    vb = v.astype(BF16)
    row = lax.broadcasted_iota(jnp.int32, (SUBLANES, hq.shape[1]), 0)
    o_sub = []
    for i in range(n_sub):
        lo = i * sub
        acc = [[jnp.zeros((SUBLANES, HG_DV), F32) for _ in range(n_rg)] for _ in range(nh)]
        for s in range(sub):
            ks, bs, vs = k[lo + s:lo + s + 1], b2[lo + s:lo + s + 1], v[lo + s:lo + s + 1]
            for rg in range(s // SUBLANES, n_rg):
                r0 = lo + rg * SUBLANES
                e = jnp.exp2(b2[r0:r0 + SUBLANES] - bs)
                if rg == s // SUBLANES:
                    e = jnp.where(row >= s % SUBLANES, e, 0.0)
                prod = q[r0:r0 + SUBLANES] * ks * e
                for h in range(nh):
                    a_col = jnp.sum(prod[:, hs[h]], axis=1, keepdims=True)
                    acc[h][rg] = acc[h][rg] + a_col * vs[:, hs[h]]
        o_i = [a[0] if n_rg == 1 else jnp.concatenate(a, axis=0) for a in acc]
        if i > 0:
            r = b[lo - 1:lo]
            qh = (q[lo:lo + sub] * jnp.exp(b[lo:lo + sub] - r)).astype(BF16)
            kh = (k[:lo] * jnp.exp(r - b[:lo])).astype(BF16)
            for h in range(nh):
                a_off = lax.dot_general(qh[:, hs[h]], kh[:, hs[h]], _NT,
                                        preferred_element_type=F32)
                o_i[h] = o_i[h] + _dot(a_off.astype(BF16), vb[:lo, hs[h]])
        o_sub.append(o_i)
    qe = (q * jnp.exp(b)).astype(BF16)
    b_last = b[c - 1:c]
    kd = (k * jnp.exp(b_last - b)).astype(BF16)
    decay = jnp.exp(b_last)
    gate = hg * _sigmoid(hg)
    outs, new_sts = [], []
    for h in range(nh):
        o = o_sub[0][h] if n_sub == 1 else jnp.concatenate([p[h] for p in o_sub], axis=0)
        o = o + lax.dot_general(qe[:, hs[h]], sts[h].astype(BF16), _NT, preferred_element_type=F32)
        new_sts.append(sts[h] * decay[:, hs[h]]
                       + lax.dot_general(vb[:, hs[h]], kd[:, hs[h]], _TN, preferred_element_type=F32))
        outs.append(_rms(o, g_out) * gate[:, hs[h]])
    return outs, new_sts


def _hgrn_lb(lb_ref):
    a = lb_ref[...]
    a0, a1 = a[0:1], a[1:2]
    m = jnp.maximum(a0, a1)
    e0, e1 = jnp.exp(a0 - m), jnp.exp(a1 - m)
    return e0 / (e0 + e1)


def _tri(c):
    return (lax.broadcasted_iota(jnp.int32, (c, c), 0)
            >= lax.broadcasted_iota(jnp.int32, (c, c), 1)).astype(F32)


def _hgrn_prompt_kernel(hq_ref, hf_ref, hi_ref, hg_ref, lb_ref, g_ref, o_ref, s_ref, st_scr,
                        *, chunk, n_chunks, heads):
    step = pl.program_id(2)

    @pl.when(step == 0)
    def _():
        st_scr[...] = jnp.zeros(st_scr.shape, F32)

    lb = _hgrn_lb(lb_ref)
    g = g_ref[...]
    tri = _tri(chunk)

    def body(ci, carry):
        rows = pl.ds(pl.multiple_of(ci * chunk, chunk), chunk)
        outs, new_sts = _hgrn_chunk(hq_ref[rows, :], hf_ref[rows, :], hi_ref[rows, :], hg_ref[rows, :],
                                    [st_scr[h] for h in range(heads)], lb, g, tri)
        o_ref[rows, :] = jnp.concatenate(outs, axis=1).astype(o_ref.dtype)
        for h in range(heads):
            st_scr[h] = new_sts[h]
        return carry

    lax.fori_loop(0, n_chunks, body, 0)

    @pl.when(step == pl.num_programs(2) - 1)
    def _():
        for h in range(heads):
            s_ref[h] = st_scr[h].T


def _hgrn_prompt(z, hg_lb, g_out, b, t):
    chunk = math.gcd(t, HG_CHUNK)
    n_chunks = min(8, t // chunk)
    tb = chunk * n_chunks
    heads = 8
    assert t % tb == 0 and HG_HEADS % heads == 0
    steps = t // tb
    w = heads * HG_DK
    col0 = (Q_LORA + KV_LORA) // w
    groups = HG_HEADS // heads

    def zspec(group):
        return pl.BlockSpec((tb, w), lambda bi, h, s: (bi * steps + s, col0 + group * groups + h))

    est = (2 * (4 * _nbytes((tb, w), F32) + _nbytes((tb, w), BF16))
           + (3 + 64) * heads * _nbytes((HG_DK, HG_DV), F32))
    return pl.pallas_call(
        functools.partial(_hgrn_prompt_kernel, chunk=chunk, n_chunks=n_chunks, heads=heads),
        grid=(b, groups, steps),
        in_specs=[zspec(0), zspec(1), zspec(2), zspec(3),
                  pl.BlockSpec((2, w), lambda bi, h, s: (0, h)),
                  pl.BlockSpec((1, HG_DV), lambda bi, h, s: (0, 0))],
        out_specs=[pl.BlockSpec((tb, w), lambda bi, h, s: (bi * steps + s, h)),
                   pl.BlockSpec((None, heads, HG_DK, HG_DV), lambda bi, h, s: (bi, h, 0, 0))],
        out_shape=[jax.ShapeDtypeStruct((b * t, HG_HEADS * HG_DV), BF16),
                   jax.ShapeDtypeStruct((b, HG_HEADS, HG_DK, HG_DV), F32)],
        scratch_shapes=[pltpu.VMEM((heads, HG_DV, HG_DK), F32)],
        compiler_params=pltpu.CompilerParams(
            dimension_semantics=("parallel", "parallel", "arbitrary"),
            vmem_limit_bytes=_vmem_limit(est)),
        name="hgrn_prompt",
    )(z, z, z, z, hg_lb, g_out)


def _hgrn_sample_kernel(hq_ref, hf_ref, hi_ref, hg_ref, s0_ref, lb_ref, g_ref, o_ref, s_ref):
    outs, new_sts = _hgrn_chunk(hq_ref[...], hf_ref[...], hi_ref[...], hg_ref[...],
                                [s0_ref[h].T for h in range(HG_HEADS)],
                                _hgrn_lb(lb_ref), g_ref[...], _tri(hq_ref.shape[0]))
    o_ref[...] = jnp.concatenate(outs, axis=1)
    for h in range(HG_HEADS):
        s_ref[h] = new_sts[h].T


def _hgrn_sample(z, state, hg_lb, g_out, t):
    nseq = state.shape[0]
    assert HG_CHUNK % t == 0 and t % SUBLANES == 0
    w = HG_HEADS * HG_DK
    col0 = (Q_LORA + KV_LORA) // w
    zspec = lambda group: pl.BlockSpec((t, w), lambda bi: (bi, col0 + group))
    sspec = pl.BlockSpec((None, HG_HEADS, HG_DK, HG_DV), lambda bi: (bi, 0, 0, 0))
    est = 2 * (5 * _nbytes((t, w), F32) + 2 * _nbytes((HG_HEADS, HG_DK, HG_DV), F32)) + 64 * _nbytes((HG_DK, HG_DV), F32)
    return pl.pallas_call(
        _hgrn_sample_kernel,
        grid=(nseq,),
        in_specs=[zspec(0), zspec(1), zspec(2), zspec(3), sspec,
                  pl.BlockSpec((2, w), lambda bi: (0, 0)),
                  pl.BlockSpec((1, HG_DV), lambda bi: (0, 0))],
        out_specs=[pl.BlockSpec((t, w), lambda bi: (bi, 0)), sspec],
        out_shape=[jax.ShapeDtypeStruct((nseq * t, w), F32),
                   jax.ShapeDtypeStruct(state.shape, F32)],
        compiler_params=pltpu.CompilerParams(
            dimension_semantics=("parallel",), vmem_limit_bytes=_vmem_limit(est)),
        name="hgrn_sample",
    )(z, z, z, z, state, hg_lb, g_out)


def _out_proj_kernel(x_ref, olat_ref, ohg_ref, wuv_ref, gm_ref, wo_ref, h_ref):
    parts = []
    for h in range(MLA_HEADS):
        om = _dot(olat_ref[:, h * KV_LORA:(h + 1) * KV_LORA], wuv_ref[h])
        parts.append(_rms(om, gm_ref[...]).astype(BF16))
    o_mla = jnp.concatenate(parts, axis=1)
    w = MLA_HEADS * V_HEAD
    h_ref[...] = (x_ref[...] + _dot(o_mla, wo_ref[:w, :])
                  + _dot(ohg_ref[...].astype(BF16), wo_ref[w:, :]))


def _out_proj(x, o_lat, o_hg, w_uvT, g_mla, w_o):
    n, d = x.shape
    tm = min(256, n)
    assert n % tm == 0
    lw = MLA_HEADS * KV_LORA
    hw = o_hg.shape[1]
    est = (2 * (2 * _nbytes((tm, d), F32) + _nbytes((tm, lw), BF16) + _nbytes((tm, hw), o_hg.dtype)
                + _nbytes(w_uvT.shape, BF16) + _nbytes(w_o.shape, BF16))
           + 4 * _nbytes((tm, d), F32))
    full = lambda shape: pl.BlockSpec(shape, lambda i: (0,) * len(shape))
    return pl.pallas_call(
        _out_proj_kernel,
        grid=(n // tm,),
        in_specs=[pl.BlockSpec((tm, d), lambda i: (i, 0)),
                  pl.BlockSpec((tm, lw), lambda i: (i, 0)),
                  pl.BlockSpec((tm, hw), lambda i: (i, 0)),
                  full(w_uvT.shape), full((1, V_HEAD)), full(w_o.shape)],
        out_specs=pl.BlockSpec((tm, d), lambda i: (i, 0)),
        out_shape=jax.ShapeDtypeStruct((n, d), F32),
        compiler_params=pltpu.CompilerParams(
            dimension_semantics=("parallel",), vmem_limit_bytes=_vmem_limit(est)),
        name="out_proj",
    )(x, o_lat, o_hg, w_uvT, g_mla, w_o)


def _ffn_kernel(h_ref, g_ref, wg_ref, wu_ref, wd_ref, o_ref, hn_scr):
    @pl.when(pl.program_id(1) == 0)
    def _():
        h = h_ref[...]
        hn_scr[...] = _rms(h, g_ref[...]).astype(BF16)
        o_ref[...] = h

    hn = hn_scr[...]
    gate = _dot(hn, wg_ref[...])
    up = _dot(hn, wu_ref[...])
    act = (gate * _sigmoid(gate) * up).astype(BF16)
    o_ref[...] += _dot(act, wd_ref[...])


def _ffn(h, g, w_gate, w_up, w_down):
    n, d = h.shape
    dff = w_gate.shape[1]
    tm = min(512, n)
    tf = 512
    assert n % tm == 0 and dff % tf == 0
    est = (2 * (2 * _nbytes((tm, d), F32) + 3 * _nbytes((d, tf), BF16))
           + _nbytes((tm, d), BF16) + 4 * _nbytes((tm, tf), F32) + _nbytes((tm, d), F32))
    return pl.pallas_call(
        _ffn_kernel,
        grid=(n // tm, dff // tf),
        in_specs=[pl.BlockSpec((tm, d), lambda i, j: (i, 0)),
                  pl.BlockSpec((1, d), lambda i, j: (0, 0)),
                  pl.BlockSpec((d, tf), lambda i, j: (0, j)),
                  pl.BlockSpec((d, tf), lambda i, j: (0, j)),
                  pl.BlockSpec((tf, d), lambda i, j: (j, 0))],
        out_specs=pl.BlockSpec((tm, d), lambda i, j: (i, 0)),
        out_shape=jax.ShapeDtypeStruct((n, d), F32),
        scratch_shapes=[pltpu.VMEM((tm, d), BF16)],
        compiler_params=pltpu.CompilerParams(
            dimension_semantics=("parallel", "arbitrary"), vmem_limit_bytes=_vmem_limit(est)),
        name="ffn",
    )(h, g, w_gate, w_up, w_down)


def _ple_final_kernel(h_ref, p_ref, gp_ref, wg_ref, wp_ref, gf_ref, y_ref):
    h = h_ref[...]
    a = _dot(_rms(h, gp_ref[...]).astype(BF16), wg_ref[...])
    pp = _dot(p_ref[...].astype(BF16), wp_ref[...])
    y_ref[...] = _rms(h + pp * _sigmoid(a), gf_ref[...])


def _ple_final(h, p, g_ple, w_gate, w_proj, g_final):
    n, d = h.shape
    pd = p.shape[1]
    tm = min(256, n)
    assert n % tm == 0
    est = (2 * (2 * _nbytes((tm, d), F32) + _nbytes((tm, pd), F32)
                + _nbytes(w_gate.shape, BF16) + _nbytes(w_proj.shape, BF16))
           + 4 * _nbytes((tm, d), F32))
    full = lambda shape: pl.BlockSpec(shape, lambda i: (0,) * len(shape))
    return pl.pallas_call(
        _ple_final_kernel,
        grid=(n // tm,),
        in_specs=[pl.BlockSpec((tm, d), lambda i: (i, 0)),
                  pl.BlockSpec((tm, pd), lambda i: (i, 0)),
                  full((1, d)), full(w_gate.shape), full(w_proj.shape), full((1, d))],
        out_specs=pl.BlockSpec((tm, d), lambda i: (i, 0)),
        out_shape=jax.ShapeDtypeStruct((n, d), F32),
        compiler_params=pltpu.CompilerParams(
            dimension_semantics=("parallel",), vmem_limit_bytes=_vmem_limit(est)),
        name="ple_final",
    )(h, p, g_ple, w_gate, w_proj, g_final)


def _rope_tables(pos, reps):
    inv = ROPE_BASE ** (-jnp.arange(0, QK_ROPE, 2, dtype=F32) / QK_ROPE)
    ang = pos.astype(F32)[:, None] * inv[None, :]
    cos, sin = jnp.cos(ang), jnp.sin(ang)
    cos128 = jnp.tile(jnp.concatenate([cos, cos], axis=1), (reps, LANES // QK_ROPE))
    sin128 = jnp.tile(jnp.concatenate([-sin, sin], axis=1), (reps, LANES // QK_ROPE))
    return cos128, sin128


def _prep_weights(g_mix, w_in, g_q, w_uq, g_kv, w_uk, w_uv, g_mla_out, g_hg_out, w_o, g_ffn,
                  w_ffn_gate, w_ffn_up, w_ffn_down, g_ple, w_ple_gate, w_ple_proj, g_final):
    row = lambda g: g.reshape(1, -1).astype(F32)
    half = QK_ROPE // 2
    lo = Q_LORA + KV_LORA
    w_main = jnp.concatenate([w_in[:, :lo], w_in[:, lo + QK_ROPE:]], axis=1).astype(BF16)
    w_k = w_in[:, lo:lo + QK_ROPE]
    lane_pad = lambda a: jnp.pad(a, [(0, 0)] * (a.ndim - 1) + [(0, LANES - a.shape[-1])])
    w_kpe = jnp.concatenate([lane_pad(w_k), lane_pad(jnp.roll(w_k, half, axis=1))], axis=1).astype(BF16)
    w_nope = w_uq[:, :, :QK_NOPE].reshape(Q_LORA, MLA_HEADS * QK_NOPE).astype(BF16)
    pe = w_uq[:, :, QK_NOPE:]
    w_pe = lane_pad(pe).reshape(Q_LORA, MLA_HEADS * LANES).astype(BF16)
    w_pe_rot = lane_pad(jnp.roll(pe, half, axis=2)).reshape(Q_LORA, MLA_HEADS * LANES).astype(BF16)
    w_ukT = jnp.transpose(w_uk, (1, 2, 0)).astype(BF16)
    w_uvT = jnp.transpose(w_uv, (1, 0, 2)).astype(BF16)
    return dict(
        g_mix=row(g_mix), w_main=w_main, w_kpe=w_kpe, g_q=row(g_q), g_kv=row(g_kv),
        w_nope=w_nope, w_pe=w_pe, w_pe_rot=w_pe_rot, w_ukT=w_ukT, w_uvT=w_uvT,
        g_mla=row(g_mla_out), g_hg=row(g_hg_out), w_o=w_o.astype(BF16), g_ffn=row(g_ffn),
        w_gate=w_ffn_gate.astype(BF16), w_up=w_ffn_up.astype(BF16), w_down=w_ffn_down.astype(BF16),
        g_ple=row(g_ple), w_ple_gate=w_ple_gate.astype(BF16), w_ple_proj=w_ple_proj.astype(BF16),
        g_final=row(g_final))


def _front(x, pos, w):
    b, t, d = x.shape
    cos128, sin128 = _rope_tables(pos, b)
    x2 = x.reshape(b * t, d)
    z, kpe, kpe_b = _in_proj(x2, w["g_mix"], cos128, sin128, w["w_main"], w["w_kpe"])
    q_cat, k_cat, ckv = _mla_q(z, kpe_b, cos128, sin128, w["g_q"], w["g_kv"],
                               w["w_nope"], w["w_pe"], w["w_pe_rot"], w["w_ukT"])
    return x2, z, q_cat, k_cat, ckv, kpe


def _back(x2, o_lat, o_hg, p, w):
    n = x2.shape[0]
    h = _out_proj(x2, o_lat, o_hg, w["w_uvT"], w["g_mla"], w["w_o"])
    h = _ffn(h, w["g_ffn"], w["w_gate"], w["w_up"], w["w_down"])
    return _ple_final(h, p.reshape(n, -1), w["g_ple"], w["w_ple_gate"], w["w_ple_proj"], w["g_final"])


def kernel(x_prompt, x_sample, cache_kv_latent, cache_k_rope, state_hgrn, page_table, p_prompt, p_sample, g_mix, w_in, g_q, w_uq, g_kv, w_uk, w_uv, g_mla_out, hg_lb, g_hg_out, w_o, g_ffn, w_ffn_gate, w_ffn_up, w_ffn_down, g_ple, w_ple_gate, w_ple_proj, g_final):
    depth = g_mix.shape[0]
    assert depth == 1 and hg_lb.shape[0] == 2, "single-layer trunk only"
    w = _prep_weights(g_mix[0], w_in[0], g_q[0], w_uq[0], g_kv[0], w_uk[0], w_uv[0], g_mla_out[0],
                      g_hg_out[0], w_o[0], g_ffn[0], w_ffn_gate[0], w_ffn_up[0], w_ffn_down[0],
                      g_ple[0], w_ple_gate[0], w_ple_proj[0], g_final)
    hg_lb = hg_lb.astype(F32)
    bp, tp, d = x_prompt.shape
    bs, ts, _ = x_sample.shape
    past_len = page_table.shape[1] * PAGE_SIZE

    x2, z, q_cat, k_cat, ckv, kpe = _front(x_prompt, jnp.arange(tp, dtype=jnp.int32), w)
    o_lat = _attn_prompt(q_cat, k_cat, bp, tp)
    o_hg, st_p = _hgrn_prompt(z, hg_lb, w["g_hg"], bp, tp)
    y_prompt = _back(x2, o_lat, o_hg, p_prompt, w).reshape(bp, tp, d)
    lat_p = ckv.reshape(1, bp, tp, KV_LORA)
    rope_p = kpe.reshape(1, bp, tp, QK_ROPE)

    pos_s = past_len + jnp.arange(ts, dtype=jnp.int32)
    x2, z, q_cat, k_cat, ckv, kpe = _front(x_sample, pos_s, w)
    n_pool = cache_kv_latent.shape[1]
    o_lat = _attn_sample(page_table, q_cat.reshape(bs, ts * MLA_HEADS, QK_CAT),
                         ckv.reshape(bs, ts, KV_LORA), kpe.reshape(bs, ts, QK_ROPE),
                         cache_kv_latent.reshape(n_pool, PAGE_SIZE, KV_LORA),
                         jnp.swapaxes(cache_k_rope.reshape(n_pool, PAGE_SIZE, QK_ROPE), 1, 2))
    o_hg, st_s = _hgrn_sample(z, state_hgrn.reshape(state_hgrn.shape[1:]), hg_lb, w["g_hg"], ts)
    y_sample = _back(x2, o_lat.reshape(bs * ts, MLA_HEADS * KV_LORA), o_hg, p_sample, w).reshape(bs, ts, d)
    lat_s = ckv.reshape(1, bs, ts, KV_LORA)
    rope_s = kpe.reshape(1, bs, ts, QK_ROPE)

    return (y_prompt, y_sample, lat_p, rope_p, st_p[None], lat_s, rope_s, st_s[None])
```
